```python
import jax, jax.numpy as jnp
from jax import lax
import numpy as np

D_MODEL = 4096
BATCH = 4
SEQ = 2048
DEPTH = 4
DEC_BATCH = 128
DEC_SEQ = 4
PAST_LEN = 16384
PAGE_SIZE = 128

CONV_W = 3
D_CONV = D_MODEL
CHUNK = 128
D_SGU = D_MODEL
SGU_GROUPS = 8
SGU_GROUP_DIM = D_SGU // SGU_GROUPS
D_FF = 11008
N_EXPERTS = 8
TOP_K = 2
D_FF_EXPERT = 14336
EPS = 1e-6
N_CONV_LAYERS = (DEPTH + 1) // 2
N_SGU_LAYERS = DEPTH // 2

kernel_name = "hybrid_shortconv_chunksgu_moe_step"


def rms_norm(x, g):
    xf = x.astype(jnp.float32)
    r = lax.rsqrt(jnp.mean(xf * xf, axis=-1, keepdims=True) + EPS)
    return (xf * r * g.astype(jnp.float32)).astype(x.dtype)


def layer_norm(x, g, b):
    xf = x.astype(jnp.float32)
    mu = jnp.mean(xf, axis=-1, keepdims=True)
    xc = xf - mu
    r = lax.rsqrt(jnp.mean(xc * xc, axis=-1, keepdims=True) + EPS)
    return (xc * r * g.astype(jnp.float32) + b.astype(jnp.float32)).astype(x.dtype)


def short_conv_mixer(h, conv_buf, w_in, taps, w_out):
    T = h.shape[1]
    bcx = h @ w_in
    b_gate, c_gate, xv = jnp.split(bcx, 3, axis=-1)
    cx = c_gate * xv
    xcat = jnp.concatenate([conv_buf.astype(cx.dtype), cx], axis=1)
    conv = taps[0] * xcat[:, 0:T]
    for k in range(1, CONV_W):
        conv = conv + taps[k] * xcat[:, k:k + T]
    y = (b_gate * conv) @ w_out
    return y, xcat[:, -(CONV_W - 1):]


def chunk_sgu_mixer(h, w_in, ln_g, ln_b, w_sp, b_sp, w_out):
    B, T, _ = h.shape
    L = min(T, CHUNK)
    nc = T // L
    uv = jax.nn.gelu(h @ w_in, approximate=False)
    u, v = jnp.split(uv, 2, axis=-1)
    v = layer_norm(v, ln_g, ln_b)
    vc = v.reshape(B, nc, L, SGU_GROUPS, SGU_GROUP_DIM)
    mask = jnp.tril(jnp.ones((L, L), dtype=bool))
    wm = jnp.where(mask[None], w_sp[:, :L, :L], 0)
    bias = jnp.transpose(b_sp[:, :L])[None, None, :, :, None]
    z = jnp.einsum('gij,bcjgd->bcigd', wm, vc) + bias
    z = z.reshape(B, T, D_SGU)
    y = (u * z) @ w_out
    return y, v


def swiglu(x, w_g, w_u, w_d):
    return (jax.nn.silu(x @ w_g) * (x @ w_u)) @ w_d


def moe_swiglu(h, w_router, w_g, w_u, w_d):
    B, T, D = h.shape
    x = h.reshape(B * T, D)
    logits = (x @ w_router).astype(jnp.float32)
    top_v, top_i = lax.top_k(logits, TOP_K)
    probs = jax.nn.softmax(top_v, axis=-1)
    gates = jnp.sum(jax.nn.one_hot(top_i, N_EXPERTS, dtype=jnp.float32) * probs[..., None], axis=1)
    out = jnp.zeros_like(x)
    for e in range(N_EXPERTS):
        out = out + gates[:, e:e + 1].astype(x.dtype) * swiglu(x, w_g[e], w_u[e], w_d[e])
    return out.reshape(B, T, D)


def run_trunk(x, conv_bufs, norm_mix_g, norm_ffn_g, norm_final_g,
              conv_w_in, conv_taps, conv_w_out,
              sgu_w_in, sgu_ln_g, sgu_ln_b, sgu_w_spatial, sgu_b_spatial, sgu_w_out,
              ffn_w_gate, ffn_w_up, ffn_w_down,
              moe_router, moe_w_gate, moe_w_up, moe_w_down):
    conv_states, sgu_vs = [], []
    h = x
    for i in range(DEPTH):
        j = i // 2
        hn = rms_norm(h, norm_mix_g[i])
        if i % 2 == 0:
            mix, st = short_conv_mixer(hn, conv_bufs[j], conv_w_in[j], conv_taps[j], conv_w_out[j])
            conv_states.append(st)
        else:
            mix, v = chunk_sgu_mixer(hn, sgu_w_in[j], sgu_ln_g[j], sgu_ln_b[j],
                                     sgu_w_spatial[j], sgu_b_spatial[j], sgu_w_out[j])
            sgu_vs.append(v)
        h = h + mix
        hn = rms_norm(h, norm_ffn_g[i])
        if i % 2 == 0:
            h = h + swiglu(hn, ffn_w_gate[j], ffn_w_up[j], ffn_w_down[j])
        else:
            h = h + moe_swiglu(hn, moe_router[j], moe_w_gate[j], moe_w_up[j], moe_w_down[j])
    return rms_norm(h, norm_final_g), jnp.stack(conv_states), jnp.stack(sgu_vs)


def setup_inputs(seed: int = 0) -> dict:
    key = jax.random.key(seed)
    ks = jax.random.split(key, 24)

    def nrm(k, shape, fan_in):
        return jax.random.normal(k, shape, jnp.float32) * (fan_in ** -0.5)

    def gain(k, shape):
        return 1.0 + 0.02 * jax.random.normal(k, shape, jnp.float32)

    return {
        "x_prompt": jax.random.normal(ks[0], (BATCH, SEQ, D_MODEL), jnp.float32),
        "x_sample": jax.random.normal(ks[1], (DEC_BATCH, DEC_SEQ, D_MODEL), jnp.float32),
        "state_conv": jax.random.normal(ks[2], (N_CONV_LAYERS, DEC_BATCH, CONV_W - 1, D_CONV), jnp.float32),
        "norm_mix_g": gain(ks[3], (DEPTH, D_MODEL)),
        "norm_ffn_g": gain(ks[4], (DEPTH, D_MODEL)),
        "norm_final_g": gain(ks[5], (D_MODEL,)),
        "conv_w_in": nrm(ks[6], (N_CONV_LAYERS, D_MODEL, 3 * D_CONV), D_MODEL),
        "conv_taps": nrm(ks[7], (N_CONV_LAYERS, CONV_W, D_CONV), CONV_W),
        "conv_w_out": nrm(ks[8], (N_CONV_LAYERS, D_CONV, D_MODEL), D_CONV),
        "sgu_w_in": nrm(ks[9], (N_SGU_LAYERS, D_MODEL, 2 * D_SGU), D_MODEL),
        "sgu_ln_g": gain(ks[10], (N_SGU_LAYERS, D_SGU)),
        "sgu_ln_b": 0.02 * jax.random.normal(ks[11], (N_SGU_LAYERS, D_SGU), jnp.float32),
        "sgu_w_spatial": nrm(ks[12], (N_SGU_LAYERS, SGU_GROUPS, CHUNK, CHUNK), CHUNK),
        "sgu_b_spatial": gain(ks[13], (N_SGU_LAYERS, SGU_GROUPS, CHUNK)),
        "sgu_w_out": nrm(ks[14], (N_SGU_LAYERS, D_SGU, D_MODEL), D_SGU),
        "ffn_w_gate": nrm(ks[15], (N_CONV_LAYERS, D_MODEL, D_FF), D_MODEL),
        "ffn_w_up": nrm(ks[16], (N_CONV_LAYERS, D_MODEL, D_FF), D_MODEL),
        "ffn_w_down": nrm(ks[17], (N_CONV_LAYERS, D_FF, D_MODEL), D_FF),
        "moe_router": nrm(ks[18], (N_SGU_LAYERS, D_MODEL, N_EXPERTS), D_MODEL),
        "moe_w_gate": nrm(ks[19], (N_SGU_LAYERS, N_EXPERTS, D_MODEL, D_FF_EXPERT), D_MODEL),
        "moe_w_up": nrm(ks[20], (N_SGU_LAYERS, N_EXPERTS, D_MODEL, D_FF_EXPERT), D_MODEL),
        "moe_w_down": nrm(ks[21], (N_SGU_LAYERS, N_EXPERTS, D_FF_EXPERT, D_MODEL), D_FF_EXPERT),
    }


def reference(x_prompt, x_sample, state_conv, norm_mix_g, norm_ffn_g, norm_final_g,
              conv_w_in, conv_taps, conv_w_out,
              sgu_w_in, sgu_ln_g, sgu_ln_b, sgu_w_spatial, sgu_b_spatial, sgu_w_out,
              ffn_w_gate, ffn_w_up, ffn_w_down,
              moe_router, moe_w_gate, moe_w_up, moe_w_down):
    zero_bufs = jnp.zeros((N_CONV_LAYERS, x_prompt.shape[0], CONV_W - 1, D_CONV), x_prompt.dtype)
    y_prompt, conv_state_prompt, _ = run_trunk(
        x_prompt, zero_bufs, norm_mix_g, norm_ffn_g, norm_final_g,
        conv_w_in, conv_taps, conv_w_out,
        sgu_w_in, sgu_ln_g, sgu_ln_b, sgu_w_spatial, sgu_b_spatial, sgu_w_out,
        ffn_w_gate, ffn_w_up, ffn_w_down,
        moe_router, moe_w_gate, moe_w_up, moe_w_down)
    y_sample, conv_state_sample, sgu_v_sample = run_trunk(
        x_sample, state_conv, norm_mix_g, norm_ffn_g, norm_final_g,
        conv_w_in, conv_taps, conv_w_out,
        sgu_w_in, sgu_ln_g, sgu_ln_b, sgu_w_spatial, sgu_b_spatial, sgu_w_out,
        ffn_w_gate, ffn_w_up, ffn_w_down,
        moe_router, moe_w_gate, moe_w_up, moe_w_down)
    return (y_prompt, y_sample, conv_state_prompt, conv_state_sample, sgu_v_sample)
```

```python
import functools
from typing import NamedTuple

import jax
import jax.numpy as jnp
from jax import lax
from jax.experimental import pallas as pl
from jax.experimental.pallas import tpu as pltpu

_EPS = 1e-6
_TOP_K = 2
_BF16 = jnp.bfloat16
_F32 = jnp.float32
_LANES = 128
_BF16_ROWS = 16
_VMEM_LIMIT = 58 * 2**20


def _params(n_axes):
    return pltpu.CompilerParams(
        dimension_semantics=("arbitrary",) * n_axes, vmem_limit_bytes=_VMEM_LIMIT)


def _div_tile(n, cap, mult):
    t = (min(cap, n) // mult) * mult
    while t > 0 and n % t:
        t -= mult
    assert t > 0, (n, cap, mult)
    return t


class _Plan(NamedTuple):
    tr: int
    r_up: int
    r_in: int
    n_sub: int
    tn: int
    r_dn: int
    c_dn: int
    tk_dn: int
    r_moe: int
    ts_moe: int
    tf_moe: int
    c_moe: int
    tk_moe: int
    tg: int
    tc: int


def _make_plan(m, d, f, fe):
    r_up = _div_tile(m, 2176, 4 * _BF16_ROWS)
    return _Plan(
        tr=_div_tile(m, 256, 8),
        r_up=r_up,
        r_in=_div_tile(m, 1088, 4 * _BF16_ROWS),
        n_sub=4,
        tn=_div_tile(d, 256, _LANES),
        r_dn=r_up,
        c_dn=_div_tile(d, 1024, _LANES),
        tk_dn=512 if f >= 512 else _LANES,
        r_moe=2560 if m >= 2560 else 256,
        ts_moe=512 if m >= 2560 else 64,
        tf_moe=_div_tile(fe, 256, _LANES),
        c_moe=_div_tile(d, 1024, _LANES),
        tk_moe=_div_tile(fe, 1024, _LANES),
        tg=256 if m >= 2560 else 64,
        tc=_div_tile(m, 128, 8),
    )


def _rms(x, g):
    r = lax.rsqrt(jnp.mean(x * x, axis=-1, keepdims=True) + _EPS)
    return x * r * g


def _rmsnorm_body(h_ref, g_ref, o_ref):
    o_ref[...] = _rms(h_ref[...], g_ref[...]).astype(o_ref.dtype)


def _rmsnorm(h, gains, layer, out_dtype, plan, row0=0, rows=None):
    m, d = h.shape
    rows = m if rows is None else rows
    tr = _div_tile(rows, plan.tr, 8)
    assert row0 % tr == 0
    b0 = row0 // tr
    return pl.pallas_call(
        _rmsnorm_body,
        grid=(rows // tr,),
        in_specs=[pl.BlockSpec((tr, d), lambda i: (b0 + i, 0)),
                  pl.BlockSpec((None, 1, d), lambda i: (layer, 0, 0))],
        out_specs=pl.BlockSpec((tr, d), lambda i: (i, 0)),
        out_shape=jax.ShapeDtypeStruct((rows, d), out_dtype),
        compiler_params=_params(1),
        name="rmsnorm",
    )(h, gains)


def _up_body(*refs, n_w, n_x, n_sub, n_valid, epilogue):
    x_ref = refs[0]
    w_refs = refs[1:1 + n_w]
    e_refs = refs[1 + n_w:1 + n_w + n_x]
    o_refs = refs[1 + n_w + n_x:]
    ts = x_ref.shape[0] // n_sub

    def compute():
        ws = [w[...].astype(_BF16) for w in w_refs]
        for s in range(n_sub):
            rows = slice(s * ts, (s + 1) * ts)
            x = x_ref[rows, :]
            accs = [jnp.dot(x, w, preferred_element_type=_F32) for w in ws]
            outs = epilogue(accs, [e[rows, :] for e in e_refs])
            for o, v in zip(o_refs, outs):
                o[rows, :] = v.astype(o.dtype)

    if n_valid is None:
        compute()
    else:
        j = pl.program_id(1)
        pl.when(j < n_valid)(compute)

        @pl.when(j >= n_valid)
        def _():
            for o in o_refs:
                o[...] = jnp.zeros(o.shape, o.dtype)


def _dense_up(x, weights, extras, out_dtypes, epilogue, *, rows, tn, n_sub, n_tiles,
              n_valid=None, name):
    m, k = x.shape
    last = n_tiles - 1 if n_valid is None else n_valid - 1

    def w_spec(layer, off):
        return pl.BlockSpec((None, k, tn), lambda i, j: (layer, 0, off + jnp.minimum(j, last)))

    in_specs = [pl.BlockSpec((rows, k), lambda i, j: (i, 0), pipeline_mode=pl.Buffered(1))]
    in_specs += [w_spec(layer, off) for _, layer, off in weights]
    in_specs += [pl.BlockSpec((rows, tn), lambda i, j: (i, j)) for _ in extras]
    body = functools.partial(_up_body, n_w=len(weights), n_x=len(extras), n_sub=n_sub,
                             n_valid=n_valid, epilogue=epilogue)
    return pl.pallas_call(
        body,
        grid=(m // rows, n_tiles),
        in_specs=in_specs,
        out_specs=[pl.BlockSpec((rows, tn), lambda i, j: (i, j)) for _ in out_dtypes],
        out_shape=[jax.ShapeDtypeStruct((m, n_tiles * tn), dt) for dt in out_dtypes],
        compiler_params=_params(2),
        name=name,
    )(x, *[w for w, _, _ in weights], *extras)


def _ep_conv_in(accs, _):
    b, c, xv = accs
    return b, c * xv


def _ep_swiglu(accs, _):
    g, u = accs
    return (jax.nn.silu(g) * u,)


def _gelu(x):
    return 0.5 * x * (1.0 + lax.erf(x * 0.7071067811865476))


def _ep_gelu2(accs, _):
    u, v = accs
    return _gelu(u), _gelu(v)


def _ep_residual(accs, extras):
    return (extras[0] + accs[0],)


def _down_body(a_ref, w_ref, res_ref, o_ref, *, k_total):
    kk = pl.program_id(2)
    tk = w_ref.shape[0]

    @pl.when(kk == 0)
    def _():
        o_ref[...] = res_ref[...]

    w = w_ref[...]
    if k_total % tk:
        row = lax.broadcasted_iota(jnp.int32, w.shape, 0)
        w = jnp.where(row < k_total - kk * tk, w, 0.0)
    o_ref[...] += jnp.dot(a_ref[...], w.astype(_BF16), preferred_element_type=_F32)


def _dense_down(a, w, layer, res, *, rows, cols, tk, k_total, name):
    m, kp = a.shape
    n = w.shape[-1]
    assert kp % tk == 0 and kp - k_total < tk
    return pl.pallas_call(
        functools.partial(_down_body, k_total=k_total),
        grid=(m // rows, n // cols, kp // tk),
        in_specs=[pl.BlockSpec((rows, tk), lambda i, c, kk: (i, kk)),
                  pl.BlockSpec((None, tk, cols), lambda i, c, kk: (layer, kk, c)),
                  pl.BlockSpec((rows, cols), lambda i, c, kk: (i, c),
                               pipeline_mode=pl.Buffered(1))],
        out_specs=pl.BlockSpec((rows, cols), lambda i, c, kk: (i, c)),
        out_shape=jax.ShapeDtypeStruct((m, n), _F32),
        compiler_params=_params(3),
        name=name,
    )(a, w, res)


def _conv_prompt_body(b_ref, cx_ref, taps_ref, g_ref, st_ref):
    x = cx_ref[...]
    t_len = x.shape[0]
    row = lax.broadcasted_iota(jnp.int32, x.shape, 0)
    x1 = jnp.where(row >= 1, pltpu.roll(x, 1, 0), 0.0)
    x2 = jnp.where(row >= 2, pltpu.roll(x, 2, 0), 0.0)
    tp = taps_ref[...]
    conv = tp[0:1] * x2 + tp[1:2] * x1 + tp[2:3] * x
    g_ref[...] = (b_ref[...] * conv).astype(g_ref.dtype)
    st_ref[...] = x[t_len - 2:t_len, :]


def _conv_sample_body(b_ref, cx_ref, hist_ref, taps_ref, _, g_ref):
    x = cx_ref[...]
    hist = hist_ref[...]
    n = x.shape[0]
    bs = hist.shape[0] // 2
    xcat = jnp.concatenate([hist, x], axis=0)
    tp = taps_ref[...]
    conv = tp[0:1] * xcat[0:n] + tp[1:2] * xcat[bs:bs + n] + tp[2:3] * xcat[2 * bs:2 * bs + n]
    g_ref[...] = (b_ref[...] * conv).astype(g_ref.dtype)


def _conv_mix(b, cx, taps, layer, hist, *, n_seq, seq, ms):
    m, d = b.shape
    mp = n_seq * seq
    ct = _div_tile(d, 512, _LANES)
    g, st = pl.pallas_call(
        _conv_prompt_body,
        grid=(n_seq, d // ct),
        in_specs=[pl.BlockSpec((seq, ct), lambda s, c: (s, c)),
                  pl.BlockSpec((seq, ct), lambda s, c: (s, c)),
                  pl.BlockSpec((None, 3, ct), lambda s, c: (layer, 0, c))],
        out_specs=[pl.BlockSpec((seq, ct), lambda s, c: (s, c)),
                   pl.BlockSpec((None, 2, ct), lambda s, c: (s, 0, c))],
        out_shape=[jax.ShapeDtypeStruct((m, d), _BF16),
                   jax.ShapeDtypeStruct((n_seq, 2, d), _F32)],
        compiler_params=_params(2),
        name="conv_prompt",
    )(b, cx, taps)
    assert mp % ms == 0
    sb = mp // ms
    g = pl.pallas_call(
        _conv_sample_body,
        grid=(d // ct,),
        in_specs=[pl.BlockSpec((ms, ct), lambda c: (sb, c)),
                  pl.BlockSpec((ms, ct), lambda c: (sb, c)),
                  pl.BlockSpec((hist.shape[0], ct), lambda c: (0, c)),
                  pl.BlockSpec((None, 3, ct), lambda c: (layer, 0, c)),
                  pl.BlockSpec(memory_space=pl.ANY)],
        out_specs=pl.BlockSpec((ms, ct), lambda c: (sb, c)),
        out_shape=jax.ShapeDtypeStruct((m, d), _BF16),
        input_output_aliases={4: 0},
        compiler_params=_params(1),
        name="conv_sample",
    )(b, cx, hist, taps, g)
    return g, st


def _layer_norm(v, g, b):
    mu = jnp.mean(v, axis=-1, keepdims=True)
    vc = v - mu
    r = lax.rsqrt(jnp.mean(vc * vc, axis=-1, keepdims=True) + _EPS)
    return vc * r * g + b


def _sgu_prompt_body(u_ref, v_ref, lg_ref, lb_ref, wm_ref, bias_ref, o_ref, *, chunk):
    vn = _layer_norm(v_ref[...], lg_ref[...], lb_ref[...]).astype(_BF16)
    n_groups = wm_ref.shape[0]
    gd = vn.shape[1] // n_groups
    for c in range(vn.shape[0] // chunk):
        rows = slice(c * chunk, (c + 1) * chunk)
        for g in range(n_groups):
            cols = slice(g * gd, (g + 1) * gd)
            z = jnp.dot(wm_ref[g], vn[rows, cols], preferred_element_type=_F32)
            z = z + bias_ref[:, cols]
            o_ref[rows, cols] = (u_ref[rows, cols] * z).astype(o_ref.dtype)


def _sgu_sample_body(u_ref, v_ref, lg_ref, lb_ref, wcol_ref, bcol_ref, _, o_ref, vn_ref,
                     vn_all):
    i = pl.program_id(0)

    @pl.when(i == 0)
    def _():
        vn_all[...] = jnp.zeros(vn_all.shape, vn_all.dtype)

    vn = _layer_norm(v_ref[...], lg_ref[...], lb_ref[...])
    vn_ref[...] = vn
    vn_all[i] = vn
    z = bcol_ref[...]
    for j in range(vn_all.shape[0]):
        z = z + wcol_ref[j:j + 1, :] * vn_all[j]
    o_ref[...] = (u_ref[...] * z).astype(o_ref.dtype)


def _sgu_mix(u, v, ln_g, ln_b, w_sp, b_sp, layer, *, mp, chunk, bs, ts):
    m, d = u.shape
    n_groups = w_sp.shape[1]
    gd = d // n_groups
    tril = jnp.tril(jnp.ones((chunk, chunk), dtype=bool))
    wm = jnp.where(tril[None], w_sp[layer], 0.0)
    bias = jnp.repeat(jnp.transpose(b_sp[layer]), gd, axis=1)
    tr = _div_tile(mp, 2 * chunk, chunk)
    uz = pl.pallas_call(
        functools.partial(_sgu_prompt_body, chunk=chunk),
        grid=(mp // tr,),
        in_specs=[pl.BlockSpec((tr, d), lambda i: (i, 0)),
                  pl.BlockSpec((tr, d), lambda i: (i, 0)),
                  pl.BlockSpec((None, 1, d), lambda i: (layer, 0, 0)),
                  pl.BlockSpec((None, 1, d), lambda i: (layer, 0, 0)),
                  pl.BlockSpec((n_groups, chunk, chunk), lambda i: (0, 0, 0)),
                  pl.BlockSpec((chunk, d), lambda i: (0, 0))],
        out_specs=pl.BlockSpec((tr, d), lambda i: (i, 0)),
        out_shape=jax.ShapeDtypeStruct((m, d), _BF16),
        compiler_params=_params(1),
        name="sgu_prompt",
    )(u, v, ln_g, ln_b, wm.astype(_BF16), bias)
    wcol = jnp.repeat(jnp.transpose(wm[:, :ts, :ts], (1, 2, 0)), gd, axis=-1)
    bcol = bias[:ts].reshape(ts, 1, d)
    assert mp % bs == 0
    sb = mp // bs
    uz, vn = pl.pallas_call(
        _sgu_sample_body,
        grid=(ts,),
        in_specs=[pl.BlockSpec((bs, d), lambda i: (sb + i, 0)),
                  pl.BlockSpec((bs, d), lambda i: (sb + i, 0)),
                  pl.BlockSpec((None, 1, d), lambda i: (layer, 0, 0)),
                  pl.BlockSpec((None, 1, d), lambda i: (layer, 0, 0)),
                  pl.BlockSpec((None, ts, d), lambda i: (i, 0, 0)),
                  pl.BlockSpec((None, 1, d), lambda i: (i, 0, 0)),
                  pl.BlockSpec(memory_space=pl.ANY)],
        out_specs=[pl.BlockSpec((bs, d), lambda i: (sb + i, 0)),
                   pl.BlockSpec((bs, d), lambda i: (i, 0))],
        out_shape=[jax.ShapeDtypeStruct((m, d), _BF16),
                   jax.ShapeDtypeStruct((ts * bs, d), _F32)],
        scratch_shapes=[pltpu.VMEM((ts, bs, d), _F32)],
        input_output_aliases={6: 0},
        compiler_params=_params(1),
        name="sgu_sample",
    )(u, v, ln_g, ln_b, wcol, bcol, uz)
    return uz, vn


def _router_body(h_ref, g_ref, wr_ref, hn_ref, idx_ref, p_ref, *, n_experts):
    hn = _rms(h_ref[...], g_ref[...])
    hn_ref[...] = hn
    logits = jnp.dot(hn.astype(_BF16), wr_ref[...].astype(_BF16), preferred_element_type=_F32)
    lane = lax.broadcasted_iota(jnp.int32, logits.shape, 1)
    neg = jnp.float32(-jnp.inf)
    logits = jnp.where(lane < n_experts, logits, neg)
    big = jnp.int32(logits.shape[1])
    m1 = jnp.max(logits, axis=-1, keepdims=True)
    i1 = jnp.min(jnp.where(logits == m1, lane, big), axis=-1, keepdims=True)
    rest = jnp.where(lane == i1, neg, logits)
    m2 = jnp.max(rest, axis=-1, keepdims=True)
    i2 = jnp.min(jnp.where(rest == m2, lane, big), axis=-1, keepdims=True)
    e2 = jnp.exp(m2 - m1)
    den = 1.0 + e2
    col = lax.broadcasted_iota(jnp.int32, idx_ref.shape, 1)
    idx_ref[...] = jnp.where(col == 0, i1, i2)
    p_ref[...] = jnp.where(col == 0, 1.0 / den, e2 / den)


def _router(h, gains, layer, w_router_padded, wr_layer, n_experts, plan):
    m, d = h.shape
    tr = plan.tr
    return pl.pallas_call(
        functools.partial(_router_body, n_experts=n_experts),
        grid=(m // tr,),
        in_specs=[pl.BlockSpec((tr, d), lambda i: (i, 0)),
                  pl.BlockSpec((None, 1, d), lambda i: (layer, 0, 0)),
                  pl.BlockSpec((None, d, _LANES), lambda i: (wr_layer, 0, 0))],
        out_specs=[pl.BlockSpec((tr, d), lambda i: (i, 0)),
                   pl.BlockSpec((tr, _TOP_K), lambda i: (i, 0)),
                   pl.BlockSpec((tr, _TOP_K), lambda i: (i, 0))],
        out_shape=[jax.ShapeDtypeStruct((m, d), _F32),
                   jax.ShapeDtypeStruct((m, _TOP_K), jnp.int32),
                   jax.ShapeDtypeStruct((m, _TOP_K), _F32)],
        compiler_params=_params(1),
        name="router",
    )(h, gains, w_router_padded)


def _dispatch_body(tok_ref, nrows_ref, src_ref, o_ref, buf, sem):
    i = pl.program_id(0)
    tg = buf.shape[0]
    base = i * tg

    @pl.when(base < nrows_ref[0])
    def _():
        def issue(r, carry):
            t = tok_ref[base + r]
            pltpu.make_async_copy(src_ref.at[pl.ds(t, 1), :], buf.at[pl.ds(r, 1), :], sem).start()
            return carry

        lax.fori_loop(0, tg, issue, 0)
        pltpu.make_async_copy(src_ref.at[pl.ds(0, tg), :], buf, sem).wait()
        o_ref[...] = buf[...].astype(o_ref.dtype)

    @pl.when(base >= nrows_ref[0])
    def _():
        o_ref[...] = jnp.zeros(o_ref.shape, o_ref.dtype)


def _dispatch(hn, row_token, n_rows_used, p_rows, plan):
    m, d = hn.shape
    tg = plan.tg
    return pl.pallas_call(
        _dispatch_body,
        grid_spec=pltpu.PrefetchScalarGridSpec(
            num_scalar_prefetch=2,
            grid=(p_rows // tg,),
            in_specs=[pl.BlockSpec(memory_space=pl.ANY)],
            out_specs=pl.BlockSpec((tg, d), lambda i, tok, nr: (i, 0)),
            scratch_shapes=[pltpu.VMEM((tg, d), _F32), pltpu.SemaphoreType.DMA(())]),
        out_shape=jax.ShapeDtypeStruct((p_rows, d), _BF16),
        compiler_params=_params(1),
        name="moe_dispatch",
    )(row_token, n_rows_used, hn)


def _moe_up_body(blk_ref, exp_ref, nsub_ref, x_ref, wg_ref, wu_ref, o_ref, *, ts):
    i = pl.program_id(0)
    nsub = nsub_ref[i]
    for n in range(1, x_ref.shape[0] // ts + 1):
        @pl.when(nsub == n)
        def _(n=n):
            x = x_ref[0:n * ts, :]
            g = jnp.dot(x, wg_ref[...].astype(_BF16), preferred_element_type=_F32)
            u = jnp.dot(x, wu_ref[...].astype(_BF16), preferred_element_type=_F32)
            o_ref[0:n * ts, :] = (jax.nn.silu(g) * u).astype(o_ref.dtype)


def _moe_up(xs, wg, wu, layer, blk, exp, nsub, plan):
    p_rows, d = xs.shape
    fe = wg.shape[-1]
    r, tf = plan.r_moe, plan.tf_moe
    nf = fe // tf

    def fcol(i, f, nsub):
        return jnp.where(nsub[i] > 0, f, nf - 1)

    return pl.pallas_call(
        functools.partial(_moe_up_body, ts=plan.ts_moe),
        grid_spec=pltpu.PrefetchScalarGridSpec(
            num_scalar_prefetch=3,
            grid=(blk.shape[0], nf),
            in_specs=[pl.BlockSpec((r, d), lambda i, f, blk, exp, nsub: (blk[i], 0),
                                   pipeline_mode=pl.Buffered(1)),
                      pl.BlockSpec((None, None, d, tf),
                                   lambda i, f, blk, exp, nsub: (layer, exp[i], 0, fcol(i, f, nsub))),
                      pl.BlockSpec((None, None, d, tf),
                                   lambda i, f, blk, exp, nsub: (layer, exp[i], 0, fcol(i, f, nsub)))],
            out_specs=pl.BlockSpec((r, tf), lambda i, f, blk, exp, nsub: (blk[i], fcol(i, f, nsub)))),
        out_shape=jax.ShapeDtypeStruct((p_rows, fe), _BF16),
        compiler_params=_params(2),
        name="moe_up",
    )(blk, exp, nsub, xs, wg, wu)


def _moe_down_body(blk_ref, exp_ref, nsub_ref, a_ref, w_ref, o_ref, *, ts):
    i = pl.program_id(0)
    kk = pl.program_id(2)
    nsub = nsub_ref[i]

    @pl.when(jnp.logical_and(kk == 0, nsub > 0))
    def _():
        o_ref[...] = jnp.zeros(o_ref.shape, o_ref.dtype)

    for n in range(1, a_ref.shape[0] // ts + 1):
        @pl.when(nsub == n)
        def _(n=n):
            o_ref[0:n * ts, :] += jnp.dot(a_ref[0:n * ts, :], w_ref[...].astype(_BF16),
                                          preferred_element_type=_F32)


def _moe_down(a, wd, layer, blk, exp, nsub, plan):
    p_rows, fe = a.shape
    d = wd.shape[-1]
    r, c, tk = plan.r_moe, plan.c_moe, plan.tk_moe
    nc, nk = d // c, fe // tk

    def live(i, v, last, nsub):
        return jnp.where(nsub[i] > 0, v, last)

    return pl.pallas_call(
        functools.partial(_moe_down_body, ts=plan.ts_moe),
        grid_spec=pltpu.PrefetchScalarGridSpec(
            num_scalar_prefetch=3,
            grid=(blk.shape[0], nc, nk),
            in_specs=[pl.BlockSpec((r, tk), lambda i, c_, k_, blk, exp, nsub:
                                   (blk[i], live(i, k_, nk - 1, nsub))),
                      pl.BlockSpec((None, None, tk, c), lambda i, c_, k_, blk, exp, nsub:
                                   (layer, exp[i], live(i, k_, nk - 1, nsub),
                                    live(i, c_, nc - 1, nsub)))],
            out_specs=pl.BlockSpec((r, c), lambda i, c_, k_, blk, exp, nsub:
                                   (blk[i], live(i, c_, nc - 1, nsub)))),
        out_shape=jax.ShapeDtypeStruct((p_rows, d), _F32),
        compiler_params=_params(3),
        name="moe_down",
    )(blk, exp, nsub, a, wd)


def _combine_body(pos_ref, h_ref, p_ref, y_ref, o_ref, buf, sem):
    i = pl.program_id(0)
    tc = h_ref.shape[0]
    base = i * tc

    def issue(r, carry):
        for k in range(_TOP_K):
            p = pos_ref[(base + r) * _TOP_K + k]
            pltpu.make_async_copy(y_ref.at[pl.ds(p, 1), :], buf.at[k, pl.ds(r, 1), :], sem).start()
        return carry

    lax.fori_loop(0, tc, issue, 0)
    for k in range(_TOP_K):
        pltpu.make_async_copy(y_ref.at[pl.ds(0, tc), :], buf.at[k], sem).wait()
    p = p_ref[...]
    o_ref[...] = h_ref[...] + (p[:, 0:1] * buf[0] + p[:, 1:2] * buf[1])


def _combine(h, gate_p, y, pos, plan):
    m, d = h.shape
    tc = plan.tc
    return pl.pallas_call(
        _combine_body,
        grid_spec=pltpu.PrefetchScalarGridSpec(
            num_scalar_prefetch=1,
            grid=(m // tc,),
            in_specs=[pl.BlockSpec((tc, d), lambda i, pos: (i, 0)),
                      pl.BlockSpec((tc, _TOP_K), lambda i, pos: (i, 0)),
                      pl.BlockSpec(memory_space=pl.ANY)],
            out_specs=pl.BlockSpec((tc, d), lambda i, pos: (i, 0)),
            scratch_shapes=[pltpu.VMEM((_TOP_K, tc, d), _F32), pltpu.SemaphoreType.DMA(())]),
        out_shape=jax.ShapeDtypeStruct((m, d), _F32),
        compiler_params=_params(1),
        name="moe_combine",
    )(pos, h, gate_p, y)


def _route(top_i, n_experts, plan, n_blocks):
    r, ts = plan.r_moe, plan.ts_moe
    e_flat = top_i.reshape(-1)
    onehot = (e_flat[:, None] == jnp.arange(n_experts, dtype=jnp.int32)[None]).astype(jnp.int32)
    csum = jnp.cumsum(onehot, axis=0)
    rank = jnp.sum((csum - onehot) * onehot, axis=1)
    counts = csum[-1]
    nblk = (counts + r - 1) // r
    blk_end = jnp.cumsum(nblk)
    blk_base = blk_end - nblk
    pos = blk_base[e_flat] * r + rank
    p_rows = n_blocks * r
    row_token = jnp.zeros((p_rows,), jnp.int32).at[pos].set(
        jnp.arange(e_flat.shape[0], dtype=jnp.int32) // _TOP_K)
    bidx = jnp.arange(n_blocks, dtype=jnp.int32)
    blk_exp = jnp.sum((bidx[:, None] >= blk_end[None]).astype(jnp.int32), axis=1)
    live = bidx < blk_end[-1]
    blk_exp = jnp.where(live, jnp.minimum(blk_exp, n_experts - 1), 0)
    in_blk = counts[blk_exp] - (bidx - blk_base[blk_exp]) * r
    nsub = jnp.where(live, jnp.clip((in_blk + ts - 1) // ts, 0, r // ts), 0)
    last = jnp.maximum(blk_end[-1] - 1, 0)
    blk_of = jnp.where(live, bidx, last)
    blk_exp = jnp.where(live, blk_exp, blk_exp[last])
    n_rows_used = (blk_end[-1] * r).reshape(1)
    return row_token, pos, blk_of, blk_exp, nsub.astype(jnp.int32), n_rows_used


def kernel(x_prompt, x_sample, state_conv, norm_mix_g, norm_ffn_g, norm_final_g, conv_w_in, conv_taps, conv_w_out, sgu_w_in, sgu_ln_g, sgu_ln_b, sgu_w_spatial, sgu_b_spatial, sgu_w_out, ffn_w_gate, ffn_w_up, ffn_w_down, moe_router, moe_w_gate, moe_w_up, moe_w_down):
    n_seq, seq, d = x_prompt.shape
    bs, ts, _ = x_sample.shape
    depth = norm_mix_g.shape[0]
    chunk = sgu_w_spatial.shape[-1]
    f = ffn_w_gate.shape[-1]
    n_experts, fe = moe_w_gate.shape[1], moe_w_gate.shape[-1]
    mp, ms = n_seq * seq, bs * ts
    m = mp + ms
    assert conv_taps.shape[1] == 3 and ts >= 2 and seq % chunk == 0 and ts <= chunk
    plan = _make_plan(m, d, f, fe)

    h = jnp.concatenate([x_prompt.reshape(mp, d),
                         jnp.transpose(x_sample, (1, 0, 2)).reshape(ms, d)], axis=0)
    g_mix = norm_mix_g.reshape(depth, 1, d)
    g_ffn = norm_ffn_g.reshape(depth, 1, d)
    g_fin = norm_final_g.reshape(1, 1, d)
    ln_g = sgu_ln_g.reshape(-1, 1, d)
    ln_b = sgu_ln_b.reshape(-1, 1, d)
    w_router = jnp.pad(moe_router, ((0, 0), (0, 0), (0, _LANES - n_experts)))
    nt = d // plan.tn
    n_blocks = (m * _TOP_K) // plan.r_moe + n_experts
    f_pad = -(-f // plan.tk_dn) * plan.tk_dn

    conv_p, conv_s, sgu_v = [], [], []
    for i in range(depth):
        j = i // 2
        hn = _rmsnorm(h, g_mix, i, _BF16, plan)
        if i % 2 == 0:
            b, cx = _dense_up(
                hn, [(conv_w_in, j, 0), (conv_w_in, j, nt), (conv_w_in, j, 2 * nt)], [],
                [_F32, _F32], _ep_conv_in, rows=plan.r_in, tn=plan.tn, n_sub=plan.n_sub,
                n_tiles=nt, name="conv_in")
            hist = jnp.transpose(state_conv[j], (1, 0, 2)).reshape(2 * bs, d)
            g, st = _conv_mix(b, cx, conv_taps, j, hist, n_seq=n_seq, seq=seq, ms=ms)
            conv_p.append(st)
            conv_s.append(jnp.transpose(cx[m - 2 * bs:].reshape(2, bs, d), (1, 0, 2)))
            (h,) = _dense_up(g, [(conv_w_out, j, 0)], [h], [_F32], _ep_residual,
                             rows=plan.r_up, tn=plan.tn, n_sub=plan.n_sub, n_tiles=nt,
                             name="conv_out")
        else:
            u, v = _dense_up(
                hn, [(sgu_w_in, j, 0), (sgu_w_in, j, nt)], [], [_F32, _F32], _ep_gelu2,
                rows=plan.r_up, tn=plan.tn, n_sub=plan.n_sub, n_tiles=nt, name="sgu_in")
            uz, vn = _sgu_mix(u, v, ln_g, ln_b, sgu_w_spatial, sgu_b_spatial, j,
                              mp=mp, chunk=chunk, bs=bs, ts=ts)
            sgu_v.append(jnp.transpose(vn.reshape(ts, bs, d), (1, 0, 2)))
            (h,) = _dense_up(uz, [(sgu_w_out, j, 0)], [h], [_F32], _ep_residual,
                             rows=plan.r_up, tn=plan.tn, n_sub=plan.n_sub, n_tiles=nt,
                             name="sgu_out")
        if i % 2 == 0:
            hn = _rmsnorm(h, g_ffn, i, _BF16, plan)
            (a,) = _dense_up(
                hn, [(ffn_w_gate, j, 0), (ffn_w_up, j, 0)], [], [_BF16], _ep_swiglu,
                rows=plan.r_up, tn=plan.tn, n_sub=plan.n_sub, n_tiles=f_pad // plan.tn,
                n_valid=f // plan.tn, name="ffn_up")
            h = _dense_down(a, ffn_w_down, j, h, rows=plan.r_dn, cols=plan.c_dn,
                            tk=plan.tk_dn, k_total=f, name="ffn_down")
        else:
            hn, top_i, top_p = _router(h, g_ffn, i, w_router, j, n_experts, plan)
            row_token, pos, blk_of, blk_exp, nsub, n_rows_used = _route(
                top_i, n_experts, plan, n_blocks)
            xs = _dispatch(hn, row_token, n_rows_used, n_blocks * plan.r_moe, plan)
            a = _moe_up(xs, moe_w_gate, moe_w_up, j, blk_of, blk_exp, nsub, plan)
            y = _moe_down(a, moe_w_down, j, blk_of, blk_exp, nsub, plan)
            h = _combine(h, top_p, y, pos, plan)

    y_prompt = _rmsnorm(h, g_fin, 0, _F32, plan, row0=0, rows=mp).reshape(n_seq, seq, d)
    y_sample = _rmsnorm(h, g_fin, 0, _F32, plan, row0=mp, rows=ms)
    y_sample = jnp.transpose(y_sample.reshape(ts, bs, d), (1, 0, 2))
    return (y_prompt, y_sample, jnp.stack(conv_p), jnp.stack(conv_s), jnp.stack(sgu_v))
```

```python
import functools
from typing import NamedTuple

import jax
import jax.numpy as jnp
from jax import lax
from jax.experimental import pallas as pl
from jax.experimental.pallas import tpu as pltpu

_EPS = 1e-6
_TOP_K = 2
_BF16 = jnp.bfloat16
_F32 = jnp.float32
_LANES = 128
_BF16_ROWS = 16
_VMEM_LIMIT = 58 * 2**20


def _params(n_axes):
    return pltpu.CompilerParams(
        dimension_semantics=("arbitrary",) * n_axes, vmem_limit_bytes=_VMEM_LIMIT)


def _div_tile(n, cap, mult):
    t = (min(cap, n) // mult) * mult
    while t > 0 and n % t:
        t -= mult
    assert t > 0, (n, cap, mult)
    return t


class _Plan(NamedTuple):
    tr: int
    r_up: int
    r_in: int
    n_sub: int
    tn: int
    r_dn: int
    c_dn: int
    tk_dn: int
    r_moe: int
    ts_moe: int
    tf_moe: int
    c_moe: int
    tk_moe: int
    tg: int
    tc: int


def _make_plan(m, ms, d, f, fe):
    r_up = _div_tile(m, 2176, 4 * _BF16_ROWS)
    return _Plan(
        tr=_div_tile(m, 256, 8),
        r_up=r_up,
        r_in=_div_tile(m, 1088, 4 * _BF16_ROWS),
        n_sub=4,
        tn=_div_tile(d, 256, _LANES),
        r_dn=r_up,
        c_dn=_div_tile(d, 1024, _LANES),
        tk_dn=1024 if f >= 1024 else _LANES,
        r_moe=2560 if m >= 2560 else 256,
        ts_moe=256 if m >= 2560 else 64,
        tf_moe=_div_tile(fe, 256, _LANES),
        c_moe=_div_tile(d, 2048, _LANES),
        tk_moe=_div_tile(fe, 1024, _LANES),
        tg=256 if m >= 2560 else 64,
        tc=_div_tile(ms, 128, 8),
    )


def _rms(x, g):
    r = lax.rsqrt(jnp.mean(x * x, axis=-1, keepdims=True) + _EPS)
    return x * r * g


def _rmsnorm_body(h_ref, g_ref, o_ref):
    o_ref[...] = _rms(h_ref[...], g_ref[...]).astype(o_ref.dtype)


def _rmsnorm(h, gains, layer, out_dtype, plan, row0=0, rows=None):
    m, d = h.shape
    rows = m if rows is None else rows
    tr = _div_tile(rows, plan.tr, 8)
    assert row0 % tr == 0
    b0 = row0 // tr
    return pl.pallas_call(
        _rmsnorm_body,
        grid=(rows // tr,),
        in_specs=[pl.BlockSpec((tr, d), lambda i: (b0 + i, 0)),
                  pl.BlockSpec((None, 1, d), lambda i: (layer, 0, 0))],
        out_specs=pl.BlockSpec((tr, d), lambda i: (i, 0)),
        out_shape=jax.ShapeDtypeStruct((rows, d), out_dtype),
        compiler_params=_params(1),
        name="rmsnorm",
    )(h, gains)


def _up_body(*refs, n_w, n_x, n_sub, n_valid, epilogue):
    x_ref = refs[0]
    w_refs = refs[1:1 + n_w]
    e_refs = refs[1 + n_w:1 + n_w + n_x]
    o_refs = refs[1 + n_w + n_x:]
    ts = x_ref.shape[0] // n_sub

    def compute():
        ws = [w[...].astype(_BF16) for w in w_refs]
        for s in range(n_sub):
            rows = slice(s * ts, (s + 1) * ts)
            x = x_ref[rows, :]
            accs = [jnp.dot(x, w, preferred_element_type=_F32) for w in ws]
            outs = epilogue(accs, [e[rows, :] for e in e_refs])
            for o, v in zip(o_refs, outs):
                o[rows, :] = v.astype(o.dtype)

    if n_valid is None:
        compute()
    else:
        j = pl.program_id(1)
        pl.when(j < n_valid)(compute)

        @pl.when(j >= n_valid)
        def _():
            for o in o_refs:
                o[...] = jnp.zeros(o.shape, o.dtype)


def _dense_up(x, weights, extras, out_dtypes, epilogue, *, rows, tn, n_sub, n_tiles,
              n_valid=None, name):
    m, k = x.shape
    last = n_tiles - 1 if n_valid is None else n_valid - 1

    def w_spec(layer, off):
        return pl.BlockSpec((None, k, tn), lambda i, j: (layer, 0, off + jnp.minimum(j, last)))

    in_specs = [pl.BlockSpec((rows, k), lambda i, j: (i, 0), pipeline_mode=pl.Buffered(1))]
    in_specs += [w_spec(layer, off) for _, layer, off in weights]
    in_specs += [pl.BlockSpec((rows, tn), lambda i, j: (i, j)) for _ in extras]
    body = functools.partial(_up_body, n_w=len(weights), n_x=len(extras), n_sub=n_sub,
                             n_valid=n_valid, epilogue=epilogue)
    return pl.pallas_call(
        body,
        grid=(m // rows, n_tiles),
        in_specs=in_specs,
        out_specs=[pl.BlockSpec((rows, tn), lambda i, j: (i, j)) for _ in out_dtypes],
        out_shape=[jax.ShapeDtypeStruct((m, n_tiles * tn), dt) for dt in out_dtypes],
        compiler_params=_params(2),
        name=name,
    )(x, *[w for w, _, _ in weights], *extras)


def _ep_conv_in(accs, _):
    b, c, xv = accs
    return b, c * xv


def _ep_swiglu(accs, _):
    g, u = accs
    return (jax.nn.silu(g) * u,)


def _gelu(x):
    return 0.5 * x * (1.0 + lax.erf(x * 0.7071067811865476))


def _ep_gelu2(accs, _):
    u, v = accs
    return _gelu(u), _gelu(v)


def _ep_residual(accs, extras):
    return (extras[0] + accs[0],)


def _down_body(a_ref, w_ref, res_ref, o_ref, *, k_total):
    kk = pl.program_id(2)
    tk = w_ref.shape[0]

    @pl.when(kk == 0)
    def _():
        o_ref[...] = res_ref[...]

    w = w_ref[...]
    if k_total % tk:
        row = lax.broadcasted_iota(jnp.int32, w.shape, 0)
        w = jnp.where(row < k_total - kk * tk, w, 0.0)
    o_ref[...] += jnp.dot(a_ref[...], w.astype(_BF16), preferred_element_type=_F32)


def _dense_down(a, w, layer, res, *, rows, cols, tk, k_total, name):
    m, kp = a.shape
    n = w.shape[-1]
    assert kp % tk == 0 and kp - k_total < tk
    return pl.pallas_call(
        functools.partial(_down_body, k_total=k_total),
        grid=(m // rows, n // cols, kp // tk),
        in_specs=[pl.BlockSpec((rows, tk), lambda i, c, kk: (i, kk)),
                  pl.BlockSpec((None, tk, cols), lambda i, c, kk: (layer, kk, c)),
                  pl.BlockSpec((rows, cols), lambda i, c, kk: (i, c),
                               pipeline_mode=pl.Buffered(1))],
        out_specs=pl.BlockSpec((rows, cols), lambda i, c, kk: (i, c)),
        out_shape=jax.ShapeDtypeStruct((m, n), _F32),
        compiler_params=_params(3),
        name=name,
    )(a, w, res)


def _conv_prompt_body(b_ref, cx_ref, taps_ref, g_ref, st_ref):
    x = cx_ref[...]
    t_len = x.shape[0]
    row = lax.broadcasted_iota(jnp.int32, x.shape, 0)
    x1 = jnp.where(row >= 1, pltpu.roll(x, 1, 0), 0.0)
    x2 = jnp.where(row >= 2, pltpu.roll(x, 2, 0), 0.0)
    tp = taps_ref[...]
    conv = tp[0:1] * x2 + tp[1:2] * x1 + tp[2:3] * x
    g_ref[...] = (b_ref[...] * conv).astype(g_ref.dtype)
    st_ref[...] = x[t_len - 2:t_len, :]


def _conv_sample_body(b_ref, cx_ref, hist_ref, taps_ref, _, g_ref):
    x = cx_ref[...]
    hist = hist_ref[...]
    n = x.shape[0]
    bs = hist.shape[0] // 2
    xcat = jnp.concatenate([hist, x], axis=0)
    tp = taps_ref[...]
    conv = tp[0:1] * xcat[0:n] + tp[1:2] * xcat[bs:bs + n] + tp[2:3] * xcat[2 * bs:2 * bs + n]
    g_ref[...] = (b_ref[...] * conv).astype(g_ref.dtype)


def _conv_mix(b, cx, taps, layer, hist, *, n_seq, seq, ms):
    m, d = b.shape
    mp = n_seq * seq
    ct = _div_tile(d, 512, _LANES)
    g, st = pl.pallas_call(
        _conv_prompt_body,
        grid=(n_seq, d // ct),
        in_specs=[pl.BlockSpec((seq, ct), lambda s, c: (s, c)),
                  pl.BlockSpec((seq, ct), lambda s, c: (s, c)),
                  pl.BlockSpec((None, 3, ct), lambda s, c: (layer, 0, c))],
        out_specs=[pl.BlockSpec((seq, ct), lambda s, c: (s, c)),
                   pl.BlockSpec((None, 2, ct), lambda s, c: (s, 0, c))],
        out_shape=[jax.ShapeDtypeStruct((m, d), _BF16),
                   jax.ShapeDtypeStruct((n_seq, 2, d), _F32)],
        compiler_params=_params(2),
        name="conv_prompt",
    )(b, cx, taps)
    assert mp % ms == 0
    sb = mp // ms
    g = pl.pallas_call(
        _conv_sample_body,
        grid=(d // ct,),
        in_specs=[pl.BlockSpec((ms, ct), lambda c: (sb, c)),
                  pl.BlockSpec((ms, ct), lambda c: (sb, c)),
                  pl.BlockSpec((hist.shape[0], ct), lambda c: (0, c)),
                  pl.BlockSpec((None, 3, ct), lambda c: (layer, 0, c)),
                  pl.BlockSpec(memory_space=pl.ANY)],
        out_specs=pl.BlockSpec((ms, ct), lambda c: (sb, c)),
        out_shape=jax.ShapeDtypeStruct((m, d), _BF16),
        input_output_aliases={4: 0},
        compiler_params=_params(1),
        name="conv_sample",
    )(b, cx, hist, taps, g)
    return g, st


def _layer_norm(v, g, b):
    mu = jnp.mean(v, axis=-1, keepdims=True)
    vc = v - mu
    r = lax.rsqrt(jnp.mean(vc * vc, axis=-1, keepdims=True) + _EPS)
    return vc * r * g + b


def _sgu_prompt_body(u_ref, v_ref, lg_ref, lb_ref, wm_ref, bias_ref, o_ref, *, chunk):
    vn = _layer_norm(v_ref[...], lg_ref[...], lb_ref[...]).astype(_BF16)
    n_groups = wm_ref.shape[0]
    gd = vn.shape[1] // n_groups
    for c in range(vn.shape[0] // chunk):
        rows = slice(c * chunk, (c + 1) * chunk)
        for g in range(n_groups):
            cols = slice(g * gd, (g + 1) * gd)
            z = jnp.dot(wm_ref[g], vn[rows, cols], preferred_element_type=_F32)
            z = z + bias_ref[:, cols]
            o_ref[rows, cols] = (u_ref[rows, cols] * z).astype(o_ref.dtype)


def _sgu_sample_body(u_ref, v_ref, lg_ref, lb_ref, wcol_ref, bcol_ref, _, o_ref, vn_ref,
                     vn_all):
    i = pl.program_id(0)

    @pl.when(i == 0)
    def _():
        vn_all[...] = jnp.zeros(vn_all.shape, vn_all.dtype)

    vn = _layer_norm(v_ref[...], lg_ref[...], lb_ref[...])
    vn_ref[...] = vn
    vn_all[i] = vn
    z = bcol_ref[...]
    for j in range(vn_all.shape[0]):
        z = z + wcol_ref[j:j + 1, :] * vn_all[j]
    o_ref[...] = (u_ref[...] * z).astype(o_ref.dtype)


def _sgu_mix(u, v, ln_g, ln_b, w_sp, b_sp, layer, *, mp, chunk, bs, ts):
    m, d = u.shape
    n_groups = w_sp.shape[1]
    gd = d // n_groups
    tril = jnp.tril(jnp.ones((chunk, chunk), dtype=bool))
    wm = jnp.where(tril[None], w_sp[layer], 0.0)
    bias = jnp.repeat(jnp.transpose(b_sp[layer]), gd, axis=1)
    tr = _div_tile(mp, 2 * chunk, chunk)
    uz = pl.pallas_call(
        functools.partial(_sgu_prompt_body, chunk=chunk),
        grid=(mp // tr,),
        in_specs=[pl.BlockSpec((tr, d), lambda i: (i, 0)),
                  pl.BlockSpec((tr, d), lambda i: (i, 0)),
                  pl.BlockSpec((None, 1, d), lambda i: (layer, 0, 0)),
                  pl.BlockSpec((None, 1, d), lambda i: (layer, 0, 0)),
                  pl.BlockSpec((n_groups, chunk, chunk), lambda i: (0, 0, 0)),
                  pl.BlockSpec((chunk, d), lambda i: (0, 0))],
        out_specs=pl.BlockSpec((tr, d), lambda i: (i, 0)),
        out_shape=jax.ShapeDtypeStruct((m, d), _BF16),
        compiler_params=_params(1),
        name="sgu_prompt",
    )(u, v, ln_g, ln_b, wm.astype(_BF16), bias)
    wcol = jnp.repeat(jnp.transpose(wm[:, :ts, :ts], (1, 2, 0)), gd, axis=-1)
    bcol = bias[:ts].reshape(ts, 1, d)
    assert mp % bs == 0
    sb = mp // bs
    uz, vn = pl.pallas_call(
        _sgu_sample_body,
        grid=(ts,),
        in_specs=[pl.BlockSpec((bs, d), lambda i: (sb + i, 0)),
                  pl.BlockSpec((bs, d), lambda i: (sb + i, 0)),
                  pl.BlockSpec((None, 1, d), lambda i: (layer, 0, 0)),
                  pl.BlockSpec((None, 1, d), lambda i: (layer, 0, 0)),
                  pl.BlockSpec((None, ts, d), lambda i: (i, 0, 0)),
                  pl.BlockSpec((None, 1, d), lambda i: (i, 0, 0)),
                  pl.BlockSpec(memory_space=pl.ANY)],
        out_specs=[pl.BlockSpec((bs, d), lambda i: (sb + i, 0)),
                   pl.BlockSpec((bs, d), lambda i: (i, 0))],
        out_shape=[jax.ShapeDtypeStruct((m, d), _BF16),
                   jax.ShapeDtypeStruct((ts * bs, d), _F32)],
        scratch_shapes=[pltpu.VMEM((ts, bs, d), _F32)],
        input_output_aliases={6: 0},
        compiler_params=_params(1),
        name="sgu_sample",
    )(u, v, ln_g, ln_b, wcol, bcol, uz)
    return uz, vn


def _pack_bf16_pairs(x_bf16):
    bits = lax.bitcast_convert_type(x_bf16.astype(_F32), jnp.uint32)
    half = bits.shape[1] // 2
    return bits[:, half:] | (bits[:, :half] >> 16)


def _unpack_bf16_pairs(w):
    lo = lax.bitcast_convert_type(w << 16, _F32)
    hi = lax.bitcast_convert_type(w & jnp.uint32(0xFFFF0000), _F32)
    return lo.astype(_BF16), hi.astype(_BF16)


def _router_body(h_ref, g_ref, wr_ref, hn_ref, idx_ref, p_ref, *, n_experts):
    hn = _rms(h_ref[...], g_ref[...]).astype(_BF16)
    hn_ref[...] = _pack_bf16_pairs(hn)
    logits = jnp.dot(hn, wr_ref[...].astype(_BF16), preferred_element_type=_F32)
    lane = lax.broadcasted_iota(jnp.int32, logits.shape, 1)
    neg = jnp.float32(-jnp.inf)
    logits = jnp.where(lane < n_experts, logits, neg)
    big = jnp.int32(logits.shape[1])
    m1 = jnp.max(logits, axis=-1, keepdims=True)
    i1 = jnp.min(jnp.where(logits == m1, lane, big), axis=-1, keepdims=True)
    rest = jnp.where(lane == i1, neg, logits)
    m2 = jnp.max(rest, axis=-1, keepdims=True)
    i2 = jnp.min(jnp.where(rest == m2, lane, big), axis=-1, keepdims=True)
    e2 = jnp.exp(m2 - m1)
    den = 1.0 + e2
    col = lax.broadcasted_iota(jnp.int32, idx_ref.shape, 1)
    idx_ref[...] = jnp.where(col == 0, i1, i2)
    p_ref[...] = jnp.where(col == 0, 1.0 / den, e2 / den)


def _router(h, gains, layer, w_router_padded, wr_layer, n_experts, plan):
    m, d = h.shape
    tr = plan.tr
    return pl.pallas_call(
        functools.partial(_router_body, n_experts=n_experts),
        grid=(m // tr,),
        in_specs=[pl.BlockSpec((tr, d), lambda i: (i, 0)),
                  pl.BlockSpec((None, 1, d), lambda i: (layer, 0, 0)),
                  pl.BlockSpec((None, d, _LANES), lambda i: (wr_layer, 0, 0))],
        out_specs=[pl.BlockSpec((tr, d // 2), lambda i: (i, 0)),
                   pl.BlockSpec((tr, _TOP_K), lambda i: (i, 0)),
                   pl.BlockSpec((tr, _TOP_K), lambda i: (i, 0))],
        out_shape=[jax.ShapeDtypeStruct((m, d // 2), jnp.uint32),
                   jax.ShapeDtypeStruct((m, _TOP_K), jnp.int32),
                   jax.ShapeDtypeStruct((m, _TOP_K), _F32)],
        compiler_params=_params(1),
        name="router",
    )(h, gains, w_router_padded)


def _dispatch_body(tok_ref, live_ref, dst_ref, src_ref, o_ref, buf, sem):
    i = pl.program_id(0)
    tg = buf.shape[0]
    base = i * tg

    @pl.when(live_ref[i] > 0)
    def _():
        def issue(r, carry):
            t = tok_ref[base + r]
            pltpu.make_async_copy(src_ref.at[pl.ds(t, 1), :], buf.at[pl.ds(r, 1), :], sem).start()
            return carry

        lax.fori_loop(0, tg, issue, 0)
        pltpu.make_async_copy(src_ref.at[pl.ds(0, tg), :], buf, sem).wait()
        lo, hi = _unpack_bf16_pairs(buf[...])
        half = buf.shape[1]
        o_ref[:, :half] = lo
        o_ref[:, half:] = hi


def _dispatch(hn_packed, row_token, tile_live, tile_dst, p_rows, plan):
    m, half = hn_packed.shape
    tg = plan.tg
    return pl.pallas_call(
        _dispatch_body,
        grid_spec=pltpu.PrefetchScalarGridSpec(
            num_scalar_prefetch=3,
            grid=(p_rows // tg,),
            in_specs=[pl.BlockSpec(memory_space=pl.ANY)],
            out_specs=pl.BlockSpec((tg, 2 * half), lambda i, tok, live, dst: (dst[i], 0)),
            scratch_shapes=[pltpu.VMEM((tg, half), jnp.uint32), pltpu.SemaphoreType.DMA(())]),
        out_shape=jax.ShapeDtypeStruct((p_rows, 2 * half), _BF16),
        compiler_params=_params(1),
        name="moe_dispatch",
    )(row_token, tile_live, tile_dst, hn_packed)


def _moe_up_body(blk_ref, exp_ref, nsub_ref, x_ref, wg_ref, wu_ref, o_ref, *, ts):
    i = pl.program_id(0)
    nsub = nsub_ref[i]
    for n in range(1, x_ref.shape[0] // ts + 1):
        @pl.when(nsub == n)
        def _(n=n):
            x = x_ref[0:n * ts, :]
            g = jnp.dot(x, wg_ref[...].astype(_BF16), preferred_element_type=_F32)
            u = jnp.dot(x, wu_ref[...].astype(_BF16), preferred_element_type=_F32)
            o_ref[0:n * ts, :] = (jax.nn.silu(g) * u).astype(o_ref.dtype)


def _moe_up(xs, wg, wu, layer, blk, exp, nsub, plan):
    p_rows, d = xs.shape
    fe = wg.shape[-1]
    r, tf = plan.r_moe, plan.tf_moe
    nf = fe // tf

    def fcol(i, f, nsub):
        return jnp.where(nsub[i] > 0, f, nf - 1)

    return pl.pallas_call(
        functools.partial(_moe_up_body, ts=plan.ts_moe),
        grid_spec=pltpu.PrefetchScalarGridSpec(
            num_scalar_prefetch=3,
            grid=(blk.shape[0], nf),
            in_specs=[pl.BlockSpec((r, d), lambda i, f, blk, exp, nsub: (blk[i], 0),
                                   pipeline_mode=pl.Buffered(1)),
                      pl.BlockSpec((None, None, d, tf),
                                   lambda i, f, blk, exp, nsub: (layer, exp[i], 0, fcol(i, f, nsub))),
                      pl.BlockSpec((None, None, d, tf),
                                   lambda i, f, blk, exp, nsub: (layer, exp[i], 0, fcol(i, f, nsub)))],
            out_specs=pl.BlockSpec((r, tf), lambda i, f, blk, exp, nsub: (blk[i], fcol(i, f, nsub)))),
        out_shape=jax.ShapeDtypeStruct((p_rows, fe), _BF16),
        compiler_params=_params(2),
        name="moe_up",
    )(blk, exp, nsub, xs, wg, wu)


def _moe_down_body(blk_ref, exp_ref, nsub_ref, a_ref, w_ref, o_ref, *, ts):
    i = pl.program_id(0)
    kk = pl.program_id(2)
    nsub = nsub_ref[i]

    @pl.when(jnp.logical_and(kk == 0, nsub > 0))
    def _():
        o_ref[...] = jnp.zeros(o_ref.shape, o_ref.dtype)

    for n in range(1, a_ref.shape[0] // ts + 1):
        @pl.when(nsub == n)
        def _(n=n):
            o_ref[0:n * ts, :] += jnp.dot(a_ref[0:n * ts, :], w_ref[...].astype(_BF16),
                                          preferred_element_type=_F32)


def _moe_down(a, wd, layer, blk, exp, nsub, plan):
    p_rows, fe = a.shape
    d = wd.shape[-1]
    r, c, tk = plan.r_moe, plan.c_moe, plan.tk_moe
    nc, nk = d // c, fe // tk

    def live(i, v, last, nsub):
        return jnp.where(nsub[i] > 0, v, last)

    return pl.pallas_call(
        functools.partial(_moe_down_body, ts=plan.ts_moe),
        grid_spec=pltpu.PrefetchScalarGridSpec(
            num_scalar_prefetch=3,
            grid=(blk.shape[0], nc, nk),
            in_specs=[pl.BlockSpec((r, tk), lambda i, c_, k_, blk, exp, nsub:
                                   (blk[i], live(i, k_, nk - 1, nsub))),
                      pl.BlockSpec((None, None, tk, c), lambda i, c_, k_, blk, exp, nsub:
                                   (layer, exp[i], live(i, k_, nk - 1, nsub),
                                    live(i, c_, nc - 1, nsub)))],
            out_specs=pl.BlockSpec((r, c), lambda i, c_, k_, blk, exp, nsub:
                                   (blk[i], live(i, c_, nc - 1, nsub)),
                                   pipeline_mode=pl.Buffered(1))),
        out_shape=jax.ShapeDtypeStruct((p_rows, d), _F32),
        compiler_params=_params(3),
        name="moe_down",
    )(blk, exp, nsub, a, wd)


def _combine_rows(pos_ref, h_ref, p_ref, y_ref, buf, sem):
    tc = h_ref.shape[0]
    base = pl.program_id(0) * tc

    def issue(r, carry):
        for k in range(_TOP_K):
            p = pos_ref[(base + r) * _TOP_K + k]
            pltpu.make_async_copy(y_ref.at[pl.ds(p, 1), :], buf.at[k, pl.ds(r, 1), :], sem).start()
        return carry

    lax.fori_loop(0, tc, issue, 0)
    for k in range(_TOP_K):
        pltpu.make_async_copy(y_ref.at[pl.ds(0, tc), :], buf.at[k], sem).wait()
    p = p_ref[...]
    return h_ref[...] + (p[:, 0:1] * buf[0] + p[:, 1:2] * buf[1])


def _combine_mid_body(pos_ref, h_ref, p_ref, g_ref, y_ref, o_ref, hn_ref, buf, sem):
    h = _combine_rows(pos_ref, h_ref, p_ref, y_ref, buf, sem)
    o_ref[...] = h
    hn_ref[...] = _rms(h, g_ref[...]).astype(hn_ref.dtype)


def _combine_last_body(pos_ref, h_ref, p_ref, g_ref, y_ref, op_ref, os_ref, buf, sem, *, n_p):
    out = _rms(_combine_rows(pos_ref, h_ref, p_ref, y_ref, buf, sem), g_ref[...])
    i = pl.program_id(0)

    @pl.when(i < n_p)
    def _():
        op_ref[...] = out

    @pl.when(i >= n_p)
    def _():
        os_ref[...] = out


def _combine(h, gate_p, y, pos, gains, layer, plan, *, mp=None):
    m, d = h.shape
    tc = plan.tc
    in_specs = [pl.BlockSpec((tc, d), lambda i, pos: (i, 0)),
                pl.BlockSpec((tc, _TOP_K), lambda i, pos: (i, 0)),
                pl.BlockSpec((None, 1, d), lambda i, pos: (layer, 0, 0)),
                pl.BlockSpec(memory_space=pl.ANY)]
    scratch = [pltpu.VMEM((_TOP_K, tc, d), _F32), pltpu.SemaphoreType.DMA(())]
    if mp is None:
        body = _combine_mid_body
        out_specs = [pl.BlockSpec((tc, d), lambda i, pos: (i, 0)),
                     pl.BlockSpec((tc, d), lambda i, pos: (i, 0))]
        out_shape = [jax.ShapeDtypeStruct((m, d), _F32), jax.ShapeDtypeStruct((m, d), _BF16)]
    else:
        n_p = mp // tc
        body = functools.partial(_combine_last_body, n_p=n_p)
        out_specs = [pl.BlockSpec((tc, d), lambda i, pos: (jnp.minimum(i, n_p - 1), 0)),
                     pl.BlockSpec((tc, d), lambda i, pos: (jnp.maximum(i - n_p, 0), 0))]
        out_shape = [jax.ShapeDtypeStruct((mp, d), _F32), jax.ShapeDtypeStruct((m - mp, d), _F32)]
    return pl.pallas_call(
        body,
        grid_spec=pltpu.PrefetchScalarGridSpec(
            num_scalar_prefetch=1, grid=(m // tc,), in_specs=in_specs, out_specs=out_specs,
            scratch_shapes=scratch),
        out_shape=out_shape,
        compiler_params=_params(1),
        name="moe_combine",
    )(pos, h, gate_p, gains, y)


def _route(top_i, n_experts, plan, n_blocks):
    r, ts, tg = plan.r_moe, plan.ts_moe, plan.tg
    e_flat = top_i.reshape(-1)
    onehot = (e_flat[:, None] == jnp.arange(n_experts, dtype=jnp.int32)[None]).astype(jnp.int32)
    csum = jnp.cumsum(onehot, axis=0)
    rank = jnp.sum((csum - onehot) * onehot, axis=1)
    counts = csum[-1]
    subs = (counts + ts - 1) // ts
    nblk = (subs + r // ts - 1) // (r // ts)
    q = ((subs + jnp.maximum(nblk, 1) - 1) // jnp.maximum(nblk, 1)) * ts
    q = jnp.maximum(q, ts)
    blk_end = jnp.cumsum(nblk)
    blk_base = blk_end - nblk
    q_a = q[e_flat]
    pos = (blk_base[e_flat] + rank // q_a) * r + rank % q_a
    p_rows = n_blocks * r
    row_token = jnp.zeros((p_rows,), jnp.int32).at[pos].set(
        jnp.arange(e_flat.shape[0], dtype=jnp.int32) // _TOP_K)
    bidx = jnp.arange(n_blocks, dtype=jnp.int32)
    blk_exp = jnp.sum((bidx[:, None] >= blk_end[None]).astype(jnp.int32), axis=1)
    live = bidx < blk_end[-1]
    blk_exp = jnp.where(live, jnp.minimum(blk_exp, n_experts - 1), 0)
    q_b = q[blk_exp]
    in_blk = jnp.clip(counts[blk_exp] - (bidx - blk_base[blk_exp]) * q_b, 0, q_b)
    nsub = jnp.where(live, (in_blk + ts - 1) // ts, 0).astype(jnp.int32)
    last = jnp.maximum(blk_end[-1] - 1, 0)
    blk_of = jnp.where(live, bidx, last)
    blk_exp = jnp.where(live, blk_exp, blk_exp[last])
    tidx = jnp.arange(p_rows // tg, dtype=jnp.int32)
    tile_live = ((tidx * tg) % r < nsub[(tidx * tg) // r] * ts).astype(jnp.int32)
    tile_dst = lax.cummax(jnp.where(tile_live > 0, tidx, 0), axis=0)
    return row_token, pos, blk_of, blk_exp, nsub, tile_live, tile_dst


def kernel(x_prompt, x_sample, state_conv, norm_mix_g, norm_ffn_g, norm_final_g, conv_w_in, conv_taps, conv_w_out, sgu_w_in, sgu_ln_g, sgu_ln_b, sgu_w_spatial, sgu_b_spatial, sgu_w_out, ffn_w_gate, ffn_w_up, ffn_w_down, moe_router, moe_w_gate, moe_w_up, moe_w_down):
    n_seq, seq, d = x_prompt.shape
    bs, ts, _ = x_sample.shape
    depth = norm_mix_g.shape[0]
    chunk = sgu_w_spatial.shape[-1]
    f = ffn_w_gate.shape[-1]
    n_experts, fe = moe_w_gate.shape[1], moe_w_gate.shape[-1]
    mp, ms = n_seq * seq, bs * ts
    m = mp + ms
    assert conv_taps.shape[1] == 3 and ts >= 2 and seq % chunk == 0 and ts <= chunk
    plan = _make_plan(m, ms, d, f, fe)

    h = jnp.concatenate([x_prompt.reshape(mp, d),
                         jnp.transpose(x_sample, (1, 0, 2)).reshape(ms, d)], axis=0)
    g_mix = norm_mix_g.reshape(depth, 1, d)
    g_ffn = norm_ffn_g.reshape(depth, 1, d)
    g_fin = norm_final_g.reshape(1, 1, d)
    ln_g = sgu_ln_g.reshape(-1, 1, d)
    ln_b = sgu_ln_b.reshape(-1, 1, d)
    w_router = jnp.pad(moe_router, ((0, 0), (0, 0), (0, _LANES - n_experts)))
    nt = d // plan.tn
    n_blocks = (m * _TOP_K) // plan.r_moe + n_experts
    f_pad = -(-f // plan.tk_dn) * plan.tk_dn

    conv_p, conv_s, sgu_v = [], [], []
    hn_next = y_final = None
    for i in range(depth):
        j = i // 2
        hn = _rmsnorm(h, g_mix, i, _BF16, plan) if hn_next is None else hn_next
        hn_next = None
        if i % 2 == 0:
            b, cx = _dense_up(
                hn, [(conv_w_in, j, 0), (conv_w_in, j, nt), (conv_w_in, j, 2 * nt)], [],
                [_F32, _F32], _ep_conv_in, rows=plan.r_in, tn=plan.tn, n_sub=plan.n_sub,
                n_tiles=nt, name="conv_in")
            hist = jnp.transpose(state_conv[j], (1, 0, 2)).reshape(2 * bs, d)
            g, st = _conv_mix(b, cx, conv_taps, j, hist, n_seq=n_seq, seq=seq, ms=ms)
            conv_p.append(st)
            conv_s.append(jnp.transpose(cx[m - 2 * bs:].reshape(2, bs, d), (1, 0, 2)))
            (h,) = _dense_up(g, [(conv_w_out, j, 0)], [h], [_F32], _ep_residual,
                             rows=plan.r_up, tn=plan.tn, n_sub=plan.n_sub, n_tiles=nt,
                             name="conv_out")
        else:
            u, v = _dense_up(
                hn, [(sgu_w_in, j, 0), (sgu_w_in, j, nt)], [], [_F32, _F32], _ep_gelu2,
                rows=plan.r_up, tn=plan.tn, n_sub=plan.n_sub, n_tiles=nt, name="sgu_in")
            uz, vn = _sgu_mix(u, v, ln_g, ln_b, sgu_w_spatial, sgu_b_spatial, j,
                              mp=mp, chunk=chunk, bs=bs, ts=ts)
            sgu_v.append(jnp.transpose(vn.reshape(ts, bs, d), (1, 0, 2)))
            (h,) = _dense_up(uz, [(sgu_w_out, j, 0)], [h], [_F32], _ep_residual,
                             rows=plan.r_up, tn=plan.tn, n_sub=plan.n_sub, n_tiles=nt,
                             name="sgu_out")
        if i % 2 == 0:
            hn = _rmsnorm(h, g_ffn, i, _BF16, plan)
            (a,) = _dense_up(
                hn, [(ffn_w_gate, j, 0), (ffn_w_up, j, 0)], [], [_BF16], _ep_swiglu,
                rows=plan.r_up, tn=plan.tn, n_sub=plan.n_sub, n_tiles=f_pad // plan.tn,
                n_valid=f // plan.tn, name="ffn_up")
            h = _dense_down(a, ffn_w_down, j, h, rows=plan.r_dn, cols=plan.c_dn,
                            tk=plan.tk_dn, k_total=f, name="ffn_down")
        else:
            hn_packed, top_i, top_p = _router(h, g_ffn, i, w_router, j, n_experts, plan)
            row_token, pos, blk_of, blk_exp, nsub, tile_live, tile_dst = _route(
                top_i, n_experts, plan, n_blocks)
            xs = _dispatch(hn_packed, row_token, tile_live, tile_dst, n_blocks * plan.r_moe, plan)
            a = _moe_up(xs, moe_w_gate, moe_w_up, j, blk_of, blk_exp, nsub, plan)
            y = _moe_down(a, moe_w_down, j, blk_of, blk_exp, nsub, plan)
            if i + 1 < depth:
                h, hn_next = _combine(h, top_p, y, pos, g_mix, i + 1, plan)
            else:
                y_final = _combine(h, top_p, y, pos, g_fin, 0, plan, mp=mp)

    if y_final is None:
        y_final = (_rmsnorm(h, g_fin, 0, _F32, plan, row0=0, rows=mp),
                   _rmsnorm(h, g_fin, 0, _F32, plan, row0=mp, rows=ms))
    y_prompt = y_final[0].reshape(n_seq, seq, d)
    y_sample = jnp.transpose(y_final[1].reshape(ts, bs, d), (1, 0, 2))
    return (y_prompt, y_sample, jnp.stack(conv_p), jnp.stack(conv_s), jnp.stack(sgu_v))
```

```python
import functools
from typing import NamedTuple

import jax
import jax.numpy as jnp
from jax import lax
from jax.experimental import pallas as pl
from jax.experimental.pallas import tpu as pltpu

_EPS = 1e-6
_TOP_K = 2
_BF16 = jnp.bfloat16
_F32 = jnp.float32
_LANES = 128
_BF16_ROWS = 16
_VMEM_LIMIT = 58 * 2**20


def _params(n_axes):
    return pltpu.CompilerParams(
        dimension_semantics=("arbitrary",) * n_axes, vmem_limit_bytes=_VMEM_LIMIT)


def _div_tile(n, cap, mult):
    t = (min(cap, n) // mult) * mult
    while t > 0 and n % t:
        t -= mult
    assert t > 0, (n, cap, mult)
    return t


class _Plan(NamedTuple):
    tr: int
    r_up: int
    r_in: int
    n_sub: int
    tn: int
    r_dn: int
    c_dn: int
    tk_dn: int
    r_moe: int
    ts_moe: int
    tf_moe: int
    c_moe: int
    tk_moe: int
    tg: int
    tc: int


def _make_plan(m, ms, d, f, fe):
    r_up = _div_tile(m, 2176, 4 * _BF16_ROWS)
    return _Plan(
        tr=_div_tile(m, 256, 8),
        r_up=r_up,
        r_in=_div_tile(m, 1088, 4 * _BF16_ROWS),
        n_sub=4,
        tn=_div_tile(d, 256, _LANES),
        r_dn=r_up,
        c_dn=_div_tile(d, 1024, _LANES),
        tk_dn=1024 if f >= 1024 else _LANES,
        r_moe=2560 if m >= 2560 else 256,
        ts_moe=512 if m >= 2560 else 64,
        tf_moe=_div_tile(fe, 256, _LANES),
        c_moe=_div_tile(d, 2048, _LANES),
        tk_moe=_div_tile(fe, 512, _LANES),
        tg=256 if m >= 2560 else 64,
        tc=_div_tile(ms, 128, 8),
    )


def _rms(x, g):
    r = lax.rsqrt(jnp.mean(x * x, axis=-1, keepdims=True) + _EPS)
    return x * r * g


def _rmsnorm_body(h_ref, g_ref, o_ref):
    o_ref[...] = _rms(h_ref[...], g_ref[...]).astype(o_ref.dtype)


def _rmsnorm(h, gains, layer, out_dtype, plan, row0=0, rows=None):
    m, d = h.shape
    rows = m if rows is None else rows
    tr = _div_tile(rows, plan.tr, 8)
    assert row0 % tr == 0
    b0 = row0 // tr
    return pl.pallas_call(
        _rmsnorm_body,
        grid=(rows // tr,),
        in_specs=[pl.BlockSpec((tr, d), lambda i: (b0 + i, 0)),
                  pl.BlockSpec((None, 1, d), lambda i: (layer, 0, 0))],
        out_specs=pl.BlockSpec((tr, d), lambda i: (i, 0)),
        out_shape=jax.ShapeDtypeStruct((rows, d), out_dtype),
        compiler_params=_params(1),
        name="rmsnorm",
    )(h, gains)


def _up_body(*refs, n_w, n_x, n_sub, n_valid, epilogue):
    x_ref = refs[0]
    w_refs = refs[1:1 + n_w]
    e_refs = refs[1 + n_w:1 + n_w + n_x]
    o_refs = refs[1 + n_w + n_x:]
    ts = x_ref.shape[0] // n_sub

    def compute():
        ws = [w[...].astype(_BF16) for w in w_refs]
        for s in range(n_sub):
            rows = slice(s * ts, (s + 1) * ts)
            x = x_ref[rows, :]
            accs = [jnp.dot(x, w, preferred_element_type=_F32) for w in ws]
            outs = epilogue(accs, [e[rows, :] for e in e_refs])
            for o, v in zip(o_refs, outs):
                o[rows, :] = v.astype(o.dtype)

    if n_valid is None:
        compute()
    else:
        j = pl.program_id(1)
        pl.when(j < n_valid)(compute)

        @pl.when(j >= n_valid)
        def _():
            for o in o_refs:
                o[...] = jnp.zeros(o.shape, o.dtype)


def _dense_up(x, weights, extras, out_dtypes, epilogue, *, rows, tn, n_sub, n_tiles,
              n_valid=None, name):
    m, k = x.shape
    last = n_tiles - 1 if n_valid is None else n_valid - 1

    def w_spec(layer, off):
        return pl.BlockSpec((None, k, tn), lambda i, j: (layer, 0, off + jnp.minimum(j, last)))

    in_specs = [pl.BlockSpec((rows, k), lambda i, j: (i, 0), pipeline_mode=pl.Buffered(1))]
    in_specs += [w_spec(layer, off) for _, layer, off in weights]
    in_specs += [pl.BlockSpec((rows, tn), lambda i, j: (i, j)) for _ in extras]
    body = functools.partial(_up_body, n_w=len(weights), n_x=len(extras), n_sub=n_sub,
                             n_valid=n_valid, epilogue=epilogue)
    return pl.pallas_call(
        body,
        grid=(m // rows, n_tiles),
        in_specs=in_specs,
        out_specs=[pl.BlockSpec((rows, tn), lambda i, j: (i, j)) for _ in out_dtypes],
        out_shape=[jax.ShapeDtypeStruct((m, n_tiles * tn), dt) for dt in out_dtypes],
        compiler_params=_params(2),
        name=name,
    )(x, *[w for w, _, _ in weights], *extras)


def _ep_conv_in(accs, _):
    b, c, xv = accs
    return b, c * xv


def _ep_swiglu(accs, _):
    g, u = accs
    return (jax.nn.silu(g) * u,)


def _gelu(x):
    return 0.5 * x * (1.0 + lax.erf(x * 0.7071067811865476))


def _ep_gelu2(accs, _):
    u, v = accs
    return _gelu(u), _gelu(v)


def _ep_residual(accs, extras):
    return (extras[0] + accs[0],)


def _down_body(a_ref, w_ref, res_ref, o_ref, *, k_total):
    kk = pl.program_id(2)
    tk = w_ref.shape[0]

    @pl.when(kk == 0)
    def _():
        o_ref[...] = res_ref[...]

    w = w_ref[...]
    if k_total % tk:
        row = lax.broadcasted_iota(jnp.int32, w.shape, 0)
        w = jnp.where(row < k_total - kk * tk, w, 0.0)
    o_ref[...] += jnp.dot(a_ref[...], w.astype(_BF16), preferred_element_type=_F32)


def _dense_down(a, w, layer, res, *, rows, cols, tk, k_total, name):
    m, kp = a.shape
    n = w.shape[-1]
    assert kp % tk == 0 and kp - k_total < tk
    return pl.pallas_call(
        functools.partial(_down_body, k_total=k_total),
        grid=(m // rows, n // cols, kp // tk),
        in_specs=[pl.BlockSpec((rows, tk), lambda i, c, kk: (i, kk)),
                  pl.BlockSpec((None, tk, cols), lambda i, c, kk: (layer, kk, c)),
                  pl.BlockSpec((rows, cols), lambda i, c, kk: (i, c),
                               pipeline_mode=pl.Buffered(1))],
        out_specs=pl.BlockSpec((rows, cols), lambda i, c, kk: (i, c)),
        out_shape=jax.ShapeDtypeStruct((m, n), _F32),
        compiler_params=_params(3),
        name=name,
    )(a, w, res)


def _conv_prompt_body(b_ref, cx_ref, taps_ref, g_ref, st_ref):
    x = cx_ref[...]
    t_len = x.shape[0]
    row = lax.broadcasted_iota(jnp.int32, x.shape, 0)
    x1 = jnp.where(row >= 1, pltpu.roll(x, 1, 0), 0.0)
    x2 = jnp.where(row >= 2, pltpu.roll(x, 2, 0), 0.0)
    tp = taps_ref[...]
    conv = tp[0:1] * x2 + tp[1:2] * x1 + tp[2:3] * x
    g_ref[...] = (b_ref[...] * conv).astype(g_ref.dtype)
    st_ref[...] = x[t_len - 2:t_len, :]


def _conv_sample_body(b_ref, cx_ref, hist_ref, taps_ref, _, g_ref):
    x = cx_ref[...]
    hist = hist_ref[...]
    n = x.shape[0]
    bs = hist.shape[0] // 2
    xcat = jnp.concatenate([hist, x], axis=0)
    tp = taps_ref[...]
    conv = tp[0:1] * xcat[0:n] + tp[1:2] * xcat[bs:bs + n] + tp[2:3] * xcat[2 * bs:2 * bs + n]
    g_ref[...] = (b_ref[...] * conv).astype(g_ref.dtype)


def _conv_mix(b, cx, taps, layer, hist, *, n_seq, seq, ms):
    m, d = b.shape
    mp = n_seq * seq
    ct = _div_tile(d, 512, _LANES)
    g, st = pl.pallas_call(
        _conv_prompt_body,
        grid=(n_seq, d // ct),
        in_specs=[pl.BlockSpec((seq, ct), lambda s, c: (s, c)),
                  pl.BlockSpec((seq, ct), lambda s, c: (s, c)),
                  pl.BlockSpec((None, 3, ct), lambda s, c: (layer, 0, c))],
        out_specs=[pl.BlockSpec((seq, ct), lambda s, c: (s, c)),
                   pl.BlockSpec((None, 2, ct), lambda s, c: (s, 0, c))],
        out_shape=[jax.ShapeDtypeStruct((m, d), _BF16),
                   jax.ShapeDtypeStruct((n_seq, 2, d), _F32)],
        compiler_params=_params(2),
        name="conv_prompt",
    )(b, cx, taps)
    assert mp % ms == 0
    sb = mp // ms
    g = pl.pallas_call(
        _conv_sample_body,
        grid=(d // ct,),
        in_specs=[pl.BlockSpec((ms, ct), lambda c: (sb, c)),
                  pl.BlockSpec((ms, ct), lambda c: (sb, c)),
                  pl.BlockSpec((hist.shape[0], ct), lambda c: (0, c)),
                  pl.BlockSpec((None, 3, ct), lambda c: (layer, 0, c)),
                  pl.BlockSpec(memory_space=pl.ANY)],
        out_specs=pl.BlockSpec((ms, ct), lambda c: (sb, c)),
        out_shape=jax.ShapeDtypeStruct((m, d), _BF16),
        input_output_aliases={4: 0},
        compiler_params=_params(1),
        name="conv_sample",
    )(b, cx, hist, taps, g)
    return g, st


def _layer_norm(v, g, b):
    mu = jnp.mean(v, axis=-1, keepdims=True)
    vc = v - mu
    r = lax.rsqrt(jnp.mean(vc * vc, axis=-1, keepdims=True) + _EPS)
    return vc * r * g + b


def _sgu_prompt_body(u_ref, v_ref, lg_ref, lb_ref, wm_ref, bias_ref, o_ref, *, chunk):
    vn = _layer_norm(v_ref[...], lg_ref[...], lb_ref[...]).astype(_BF16)
    n_groups = wm_ref.shape[0]
    gd = vn.shape[1] // n_groups
    for c in range(vn.shape[0] // chunk):
        rows = slice(c * chunk, (c + 1) * chunk)
        for g in range(n_groups):
            cols = slice(g * gd, (g + 1) * gd)
            z = jnp.dot(wm_ref[g], vn[rows, cols], preferred_element_type=_F32)
            z = z + bias_ref[:, cols]
            o_ref[rows, cols] = (u_ref[rows, cols] * z).astype(o_ref.dtype)


def _sgu_sample_body(u_ref, v_ref, lg_ref, lb_ref, wcol_ref, bcol_ref, _, o_ref, vn_ref,
                     vn_all):
    i = pl.program_id(0)

    @pl.when(i == 0)
    def _():
        vn_all[...] = jnp.zeros(vn_all.shape, vn_all.dtype)

    vn = _layer_norm(v_ref[...], lg_ref[...], lb_ref[...])
    vn_ref[...] = vn
    vn_all[i] = vn
    z = bcol_ref[...]
    for j in range(vn_all.shape[0]):
        z = z + wcol_ref[j:j + 1, :] * vn_all[j]
    o_ref[...] = (u_ref[...] * z).astype(o_ref.dtype)


def _sgu_mix(u, v, ln_g, ln_b, w_sp, b_sp, layer, *, mp, chunk, bs, ts):
    m, d = u.shape
    n_groups = w_sp.shape[1]
    gd = d // n_groups
    tril = jnp.tril(jnp.ones((chunk, chunk), dtype=bool))
    wm = jnp.where(tril[None], w_sp[layer], 0.0)
    bias = jnp.repeat(jnp.transpose(b_sp[layer]), gd, axis=1)
    tr = _div_tile(mp, 2 * chunk, chunk)
    uz = pl.pallas_call(
        functools.partial(_sgu_prompt_body, chunk=chunk),
        grid=(mp // tr,),
        in_specs=[pl.BlockSpec((tr, d), lambda i: (i, 0)),
                  pl.BlockSpec((tr, d), lambda i: (i, 0)),
                  pl.BlockSpec((None, 1, d), lambda i: (layer, 0, 0)),
                  pl.BlockSpec((None, 1, d), lambda i: (layer, 0, 0)),
                  pl.BlockSpec((n_groups, chunk, chunk), lambda i: (0, 0, 0)),
                  pl.BlockSpec((chunk, d), lambda i: (0, 0))],
        out_specs=pl.BlockSpec((tr, d), lambda i: (i, 0)),
        out_shape=jax.ShapeDtypeStruct((m, d), _BF16),
        compiler_params=_params(1),
        name="sgu_prompt",
    )(u, v, ln_g, ln_b, wm.astype(_BF16), bias)
    wcol = jnp.repeat(jnp.transpose(wm[:, :ts, :ts], (1, 2, 0)), gd, axis=-1)
    bcol = bias[:ts].reshape(ts, 1, d)
    assert mp % bs == 0
    sb = mp // bs
    uz, vn = pl.pallas_call(
        _sgu_sample_body,
        grid=(ts,),
        in_specs=[pl.BlockSpec((bs, d), lambda i: (sb + i, 0)),
                  pl.BlockSpec((bs, d), lambda i: (sb + i, 0)),
                  pl.BlockSpec((None, 1, d), lambda i: (layer, 0, 0)),
                  pl.BlockSpec((None, 1, d), lambda i: (layer, 0, 0)),
                  pl.BlockSpec((None, ts, d), lambda i: (i, 0, 0)),
                  pl.BlockSpec((None, 1, d), lambda i: (i, 0, 0)),
                  pl.BlockSpec(memory_space=pl.ANY)],
        out_specs=[pl.BlockSpec((bs, d), lambda i: (sb + i, 0)),
                   pl.BlockSpec((bs, d), lambda i: (i, 0))],
        out_shape=[jax.ShapeDtypeStruct((m, d), _BF16),
                   jax.ShapeDtypeStruct((ts * bs, d), _F32)],
        scratch_shapes=[pltpu.VMEM((ts, bs, d), _F32)],
        input_output_aliases={6: 0},
        compiler_params=_params(1),
        name="sgu_sample",
    )(u, v, ln_g, ln_b, wcol, bcol, uz)
    return uz, vn


def _pack_bf16_pairs(x_bf16):
    bits = lax.bitcast_convert_type(x_bf16.astype(_F32), jnp.uint32)
    half = bits.shape[1] // 2
    return bits[:, half:] | (bits[:, :half] >> 16)


def _unpack_bf16_pairs(w):
    lo = lax.bitcast_convert_type(w << 16, _F32)
    hi = lax.bitcast_convert_type(w & jnp.uint32(0xFFFF0000), _F32)
    return lo.astype(_BF16), hi.astype(_BF16)


def _router_body(h_ref, g_ref, wr_ref, hn_ref, idx_ref, p_ref, *, n_experts):
    hn = _rms(h_ref[...], g_ref[...]).astype(_BF16)
    hn_ref[...] = _pack_bf16_pairs(hn)
    logits = jnp.dot(hn, wr_ref[...].astype(_BF16), preferred_element_type=_F32)
    lane = lax.broadcasted_iota(jnp.int32, logits.shape, 1)
    neg = jnp.float32(-jnp.inf)
    logits = jnp.where(lane < n_experts, logits, neg)
    big = jnp.int32(logits.shape[1])
    m1 = jnp.max(logits, axis=-1, keepdims=True)
    i1 = jnp.min(jnp.where(logits == m1, lane, big), axis=-1, keepdims=True)
    rest = jnp.where(lane == i1, neg, logits)
    m2 = jnp.max(rest, axis=-1, keepdims=True)
    i2 = jnp.min(jnp.where(rest == m2, lane, big), axis=-1, keepdims=True)
    e2 = jnp.exp(m2 - m1)
    den = 1.0 + e2
    col = lax.broadcasted_iota(jnp.int32, idx_ref.shape, 1)
    idx_ref[...] = jnp.where(col == 0, i1, i2)
    p_ref[...] = jnp.where(col == 0, 1.0 / den, e2 / den)


def _router(h, gains, layer, w_router_padded, wr_layer, n_experts, plan):
    m, d = h.shape
    tr = plan.tr
    return pl.pallas_call(
        functools.partial(_router_body, n_experts=n_experts),
        grid=(m // tr,),
        in_specs=[pl.BlockSpec((tr, d), lambda i: (i, 0)),
                  pl.BlockSpec((None, 1, d), lambda i: (layer, 0, 0)),
                  pl.BlockSpec((None, d, _LANES), lambda i: (wr_layer, 0, 0))],
        out_specs=[pl.BlockSpec((tr, d // 2), lambda i: (i, 0)),
                   pl.BlockSpec((tr, _TOP_K), lambda i: (i, 0)),
                   pl.BlockSpec((tr, _TOP_K), lambda i: (i, 0))],
        out_shape=[jax.ShapeDtypeStruct((m, d // 2), jnp.uint32),
                   jax.ShapeDtypeStruct((m, _TOP_K), jnp.int32),
                   jax.ShapeDtypeStruct((m, _TOP_K), _F32)],
        compiler_params=_params(1),
        name="router",
    )(h, gains, w_router_padded)


def _dispatch_body(tok_ref, live_ref, dst_ref, src_ref, o_ref, buf, sem):
    i = pl.program_id(0)
    tg = buf.shape[0]
    base = i * tg

    @pl.when(live_ref[i] > 0)
    def _():
        def issue(r, carry):
            t = tok_ref[base + r]
            pltpu.make_async_copy(src_ref.at[pl.ds(t, 1), :], buf.at[pl.ds(r, 1), :], sem).start()
            return carry

        lax.fori_loop(0, tg, issue, 0)
        pltpu.make_async_copy(src_ref.at[pl.ds(0, tg), :], buf, sem).wait()
        lo, hi = _unpack_bf16_pairs(buf[...])
        half = buf.shape[1]
        o_ref[:, :half] = lo
        o_ref[:, half:] = hi


def _dispatch(hn_packed, row_token, tile_live, tile_dst, p_rows, plan):
    m, half = hn_packed.shape
    tg = plan.tg
    return pl.pallas_call(
        _dispatch_body,
        grid_spec=pltpu.PrefetchScalarGridSpec(
            num_scalar_prefetch=3,
            grid=(p_rows // tg,),
            in_specs=[pl.BlockSpec(memory_space=pl.ANY)],
            out_specs=pl.BlockSpec((tg, 2 * half), lambda i, tok, live, dst: (dst[i], 0)),
            scratch_shapes=[pltpu.VMEM((tg, half), jnp.uint32), pltpu.SemaphoreType.DMA(())]),
        out_shape=jax.ShapeDtypeStruct((p_rows, 2 * half), _BF16),
        compiler_params=_params(1),
        name="moe_dispatch",
    )(row_token, tile_live, tile_dst, hn_packed)


def _moe_up_body(blk_ref, exp_ref, nsub_ref, x_ref, wg_ref, wu_ref, o_ref, *, ts):
    i = pl.program_id(0)
    nsub = nsub_ref[i]
    for n in range(1, x_ref.shape[0] // ts + 1):
        @pl.when(nsub == n)
        def _(n=n):
            x = x_ref[0:n * ts, :]
            g = jnp.dot(x, wg_ref[...].astype(_BF16), preferred_element_type=_F32)
            u = jnp.dot(x, wu_ref[...].astype(_BF16), preferred_element_type=_F32)
            o_ref[0:n * ts, :] = (jax.nn.silu(g) * u).astype(o_ref.dtype)


def _moe_up(xs, wg, wu, layer, blk, exp, nsub, plan):
    p_rows, d = xs.shape
    fe = wg.shape[-1]
    r, tf = plan.r_moe, plan.tf_moe
    nf = fe // tf

    def fcol(i, f, nsub):
        return jnp.where(nsub[i] > 0, f, nf - 1)

    return pl.pallas_call(
        functools.partial(_moe_up_body, ts=plan.ts_moe),
        grid_spec=pltpu.PrefetchScalarGridSpec(
            num_scalar_prefetch=3,
            grid=(blk.shape[0], nf),
            in_specs=[pl.BlockSpec((r, d), lambda i, f, blk, exp, nsub: (blk[i], 0),
                                   pipeline_mode=pl.Buffered(1)),
                      pl.BlockSpec((None, None, d, tf),
                                   lambda i, f, blk, exp, nsub: (layer, exp[i], 0, fcol(i, f, nsub))),
                      pl.BlockSpec((None, None, d, tf),
                                   lambda i, f, blk, exp, nsub: (layer, exp[i], 0, fcol(i, f, nsub)))],
            out_specs=pl.BlockSpec((r, tf), lambda i, f, blk, exp, nsub: (blk[i], fcol(i, f, nsub)))),
        out_shape=jax.ShapeDtypeStruct((p_rows, fe), _BF16),
        compiler_params=_params(2),
        name="moe_up",
    )(blk, exp, nsub, xs, wg, wu)


def _moe_down_body(blk_ref, exp_ref, nsub_ref, a_ref, w_ref, o_ref, *, ts):
    i = pl.program_id(0)
    kk = pl.program_id(2)
    nsub = nsub_ref[i]

    @pl.when(jnp.logical_and(kk == 0, nsub > 0))
    def _():
        o_ref[...] = jnp.zeros(o_ref.shape, o_ref.dtype)

    for n in range(1, a_ref.shape[0] // ts + 1):
        @pl.when(nsub == n)
        def _(n=n):
            o_ref[0:n * ts, :] += jnp.dot(a_ref[0:n * ts, :], w_ref[...].astype(_BF16),
                                          preferred_element_type=_F32)


def _moe_down(a, wd, layer, blk, exp, nsub, plan):
    p_rows, fe = a.shape
    d = wd.shape[-1]
    r, c, tk = plan.r_moe, plan.c_moe, plan.tk_moe
    nc, nk = d // c, fe // tk

    def live(i, v, last, nsub):
        return jnp.where(nsub[i] > 0, v, last)

    return pl.pallas_call(
        functools.partial(_moe_down_body, ts=plan.ts_moe),
        grid_spec=pltpu.PrefetchScalarGridSpec(
            num_scalar_prefetch=3,
            grid=(blk.shape[0], nc, nk),
            in_specs=[pl.BlockSpec((r, tk), lambda i, c_, k_, blk, exp, nsub:
                                   (blk[i], live(i, k_, nk - 1, nsub))),
                      pl.BlockSpec((None, None, tk, c), lambda i, c_, k_, blk, exp, nsub:
                                   (layer, exp[i], live(i, k_, nk - 1, nsub),
                                    live(i, c_, nc - 1, nsub)))],
            out_specs=pl.BlockSpec((r, c), lambda i, c_, k_, blk, exp, nsub:
                                   (blk[i], live(i, c_, nc - 1, nsub)),
                                   pipeline_mode=pl.Buffered(1))),
        out_shape=jax.ShapeDtypeStruct((p_rows, d), _F32),
        compiler_params=_params(3),
        name="moe_down",
    )(blk, exp, nsub, a, wd)


def _combine_rows(pos_ref, h_ref, p_ref, y_ref, buf, sem):
    tc = h_ref.shape[0]
    base = pl.program_id(0) * tc

    def issue(r, carry):
        for k in range(_TOP_K):
            p = pos_ref[(base + r) * _TOP_K + k]
            pltpu.make_async_copy(y_ref.at[pl.ds(p, 1), :], buf.at[k, pl.ds(r, 1), :], sem).start()
        return carry

    lax.fori_loop(0, tc, issue, 0)
    for k in range(_TOP_K):
        pltpu.make_async_copy(y_ref.at[pl.ds(0, tc), :], buf.at[k], sem).wait()
    p = p_ref[...]
    return h_ref[...] + (p[:, 0:1] * buf[0] + p[:, 1:2] * buf[1])


def _combine_mid_body(pos_ref, h_ref, p_ref, g_ref, y_ref, o_ref, hn_ref, buf, sem):
    h = _combine_rows(pos_ref, h_ref, p_ref, y_ref, buf, sem)
    o_ref[...] = h
    hn_ref[...] = _rms(h, g_ref[...]).astype(hn_ref.dtype)


def _combine_last_body(pos_ref, h_ref, p_ref, g_ref, y_ref, op_ref, os_ref, buf, sem, *, n_p):
    out = _rms(_combine_rows(pos_ref, h_ref, p_ref, y_ref, buf, sem), g_ref[...])
    i = pl.program_id(0)

    @pl.when(i < n_p)
    def _():
        op_ref[...] = out

    @pl.when(i >= n_p)
    def _():
        os_ref[...] = out


def _combine(h, gate_p, y, pos, gains, layer, plan, *, mp=None):
    m, d = h.shape
    tc = plan.tc
    in_specs = [pl.BlockSpec((tc, d), lambda i, pos: (i, 0)),
                pl.BlockSpec((tc, _TOP_K), lambda i, pos: (i, 0)),
                pl.BlockSpec((None, 1, d), lambda i, pos: (layer, 0, 0)),
                pl.BlockSpec(memory_space=pl.ANY)]
    scratch = [pltpu.VMEM((_TOP_K, tc, d), _F32), pltpu.SemaphoreType.DMA(())]
    if mp is None:
        body = _combine_mid_body
        out_specs = [pl.BlockSpec((tc, d), lambda i, pos: (i, 0)),
                     pl.BlockSpec((tc, d), lambda i, pos: (i, 0))]
        out_shape = [jax.ShapeDtypeStruct((m, d), _F32), jax.ShapeDtypeStruct((m, d), _BF16)]
    else:
        n_p = mp // tc
        body = functools.partial(_combine_last_body, n_p=n_p)
        out_specs = [pl.BlockSpec((tc, d), lambda i, pos: (jnp.minimum(i, n_p - 1), 0)),
                     pl.BlockSpec((tc, d), lambda i, pos: (jnp.maximum(i - n_p, 0), 0))]
        out_shape = [jax.ShapeDtypeStruct((mp, d), _F32), jax.ShapeDtypeStruct((m - mp, d), _F32)]
    return pl.pallas_call(
        body,
        grid_spec=pltpu.PrefetchScalarGridSpec(
            num_scalar_prefetch=1, grid=(m // tc,), in_specs=in_specs, out_specs=out_specs,
            scratch_shapes=scratch),
        out_shape=out_shape,
        compiler_params=_params(1),
        name="moe_combine",
    )(pos, h, gate_p, gains, y)


def _route(top_i, n_experts, plan, n_blocks):
    r, ts, tg = plan.r_moe, plan.ts_moe, plan.tg
    e_flat = top_i.reshape(-1)
    onehot = (e_flat[:, None] == jnp.arange(n_experts, dtype=jnp.int32)[None]).astype(jnp.int32)
    csum = jnp.cumsum(onehot, axis=0)
    rank = jnp.sum((csum - onehot) * onehot, axis=1)
    counts = csum[-1]
    subs = (counts + ts - 1) // ts
    nblk = (subs + r // ts - 1) // (r // ts)
    q = ((subs + jnp.maximum(nblk, 1) - 1) // jnp.maximum(nblk, 1)) * ts
    q = jnp.maximum(q, ts)
    blk_end = jnp.cumsum(nblk)
    blk_base = blk_end - nblk
    q_a = q[e_flat]
    pos = (blk_base[e_flat] + rank // q_a) * r + rank % q_a
    p_rows = n_blocks * r
    row_token = jnp.zeros((p_rows,), jnp.int32).at[pos].set(
        jnp.arange(e_flat.shape[0], dtype=jnp.int32) // _TOP_K)
    bidx = jnp.arange(n_blocks, dtype=jnp.int32)
    blk_exp = jnp.sum((bidx[:, None] >= blk_end[None]).astype(jnp.int32), axis=1)
    live = bidx < blk_end[-1]
    blk_exp = jnp.where(live, jnp.minimum(blk_exp, n_experts - 1), 0)
    q_b = q[blk_exp]
    in_blk = jnp.clip(counts[blk_exp] - (bidx - blk_base[blk_exp]) * q_b, 0, q_b)
    nsub = jnp.where(live, (in_blk + ts - 1) // ts, 0).astype(jnp.int32)
    last = jnp.maximum(blk_end[-1] - 1, 0)
    blk_of = jnp.where(live, bidx, last)
    blk_exp = jnp.where(live, blk_exp, blk_exp[last])
    tidx = jnp.arange(p_rows // tg, dtype=jnp.int32)
    tile_live = ((tidx * tg) % r < nsub[(tidx * tg) // r] * ts).astype(jnp.int32)
    tile_dst = lax.cummax(jnp.where(tile_live > 0, tidx, 0), axis=0)
    return row_token, pos, blk_of, blk_exp, nsub, tile_live, tile_dst


def kernel(x_prompt, x_sample, state_conv, norm_mix_g, norm_ffn_g, norm_final_g, conv_w_in, conv_taps, conv_w_out, sgu_w_in, sgu_ln_g, sgu_ln_b, sgu_w_spatial, sgu_b_spatial, sgu_w_out, ffn_w_gate, ffn_w_up, ffn_w_down, moe_router, moe_w_gate, moe_w_up, moe_w_down):
    n_seq, seq, d = x_prompt.shape
    bs, ts, _ = x_sample.shape
    depth = norm_mix_g.shape[0]
    chunk = sgu_w_spatial.shape[-1]
    f = ffn_w_gate.shape[-1]
    n_experts, fe = moe_w_gate.shape[1], moe_w_gate.shape[-1]
    mp, ms = n_seq * seq, bs * ts
    m = mp + ms
    assert conv_taps.shape[1] == 3 and ts >= 2 and seq % chunk == 0 and ts <= chunk
    plan = _make_plan(m, ms, d, f, fe)

    h = jnp.concatenate([x_prompt.reshape(mp, d),
                         jnp.transpose(x_sample, (1, 0, 2)).reshape(ms, d)], axis=0)
    g_mix = norm_mix_g.reshape(depth, 1, d)
    g_ffn = norm_ffn_g.reshape(depth, 1, d)
    g_fin = norm_final_g.reshape(1, 1, d)
    ln_g = sgu_ln_g.reshape(-1, 1, d)
    ln_b = sgu_ln_b.reshape(-1, 1, d)
    w_router = jnp.pad(moe_router, ((0, 0), (0, 0), (0, _LANES - n_experts)))
    nt = d // plan.tn
    n_blocks = (m * _TOP_K) // plan.r_moe + n_experts
    f_pad = -(-f // plan.tk_dn) * plan.tk_dn

    conv_p, conv_s, sgu_v = [], [], []
    hn_next = y_final = None
    for i in range(depth):
        j = i // 2
        hn = _rmsnorm(h, g_mix, i, _BF16, plan) if hn_next is None else hn_next
        hn_next = None
        if i % 2 == 0:
            b, cx = _dense_up(
                hn, [(conv_w_in, j, 0), (conv_w_in, j, nt), (conv_w_in, j, 2 * nt)], [],
                [_F32, _F32], _ep_conv_in, rows=plan.r_in, tn=plan.tn, n_sub=plan.n_sub,
                n_tiles=nt, name="conv_in")
            hist = jnp.transpose(state_conv[j], (1, 0, 2)).reshape(2 * bs, d)
            g, st = _conv_mix(b, cx, conv_taps, j, hist, n_seq=n_seq, seq=seq, ms=ms)
            conv_p.append(st)
            conv_s.append(jnp.transpose(cx[m - 2 * bs:].reshape(2, bs, d), (1, 0, 2)))
            (h,) = _dense_up(g, [(conv_w_out, j, 0)], [h], [_F32], _ep_residual,
                             rows=plan.r_up, tn=plan.tn, n_sub=plan.n_sub, n_tiles=nt,
                             name="conv_out")
        else:
            u, v = _dense_up(
                hn, [(sgu_w_in, j, 0), (sgu_w_in, j, nt)], [], [_F32, _F32], _ep_gelu2,
                rows=plan.r_up, tn=plan.tn, n_sub=plan.n_sub, n_tiles=nt, name="sgu_in")
            uz, vn = _sgu_mix(u, v, ln_g, ln_b, sgu_w_spatial, sgu_b_spatial, j,
                              mp=mp, chunk=chunk, bs=bs, ts=ts)
            sgu_v.append(jnp.transpose(vn.reshape(ts, bs, d), (1, 0, 2)))
            (h,) = _dense_up(uz, [(sgu_w_out, j, 0)], [h], [_F32], _ep_residual,
                             rows=plan.r_up, tn=plan.tn, n_sub=plan.n_sub, n_tiles=nt,
                             name="sgu_out")
        if i % 2 == 0:
            hn = _rmsnorm(h, g_ffn, i, _BF16, plan)
            (a,) = _dense_up(
                hn, [(ffn_w_gate, j, 0), (ffn_w_up, j, 0)], [], [_BF16], _ep_swiglu,
                rows=plan.r_up, tn=plan.tn, n_sub=plan.n_sub, n_tiles=f_pad // plan.tn,
                n_valid=f // plan.tn, name="ffn_up")
            h = _dense_down(a, ffn_w_down, j, h, rows=plan.r_dn, cols=plan.c_dn,
                            tk=plan.tk_dn, k_total=f, name="ffn_down")
        else:
            hn_packed, top_i, top_p = _router(h, g_ffn, i, w_router, j, n_experts, plan)
            row_token, pos, blk_of, blk_exp, nsub, tile_live, tile_dst = _route(
                top_i, n_experts, plan, n_blocks)
            xs = _dispatch(hn_packed, row_token, tile_live, tile_dst, n_blocks * plan.r_moe, plan)
            a = _moe_up(xs, moe_w_gate, moe_w_up, j, blk_of, blk_exp, nsub, plan)
            y = _moe_down(a, moe_w_down, j, blk_of, blk_exp, nsub, plan)
            if i + 1 < depth:
                h, hn_next = _combine(h, top_p, y, pos, g_mix, i + 1, plan)
            else:
                y_final = _combine(h, top_p, y, pos, g_fin, 0, plan, mp=mp)

    if y_final is None:
        y_final = (_rmsnorm(h, g_fin, 0, _F32, plan, row0=0, rows=mp),
                   _rmsnorm(h, g_fin, 0, _F32, plan, row0=mp, rows=ms))
    y_prompt = y_final[0].reshape(n_seq, seq, d)
    y_sample = jnp.transpose(y_final[1].reshape(ts, bs, d), (1, 0, 2))
    return (y_prompt, y_sample, jnp.stack(conv_p), jnp.stack(conv_s), jnp.stack(sgu_v))
```

```python
import functools
from typing import NamedTuple

import jax
import jax.numpy as jnp
from jax import lax
from jax.experimental import pallas as pl
from jax.experimental.pallas import tpu as pltpu

_EPS = 1e-6
_TOP_K = 2
_BF16 = jnp.bfloat16
_F32 = jnp.float32
_LANES = 128
_BF16_ROWS = 16
_VMEM_LIMIT = 58 * 2**20


def _params(n_axes):
    return pltpu.CompilerParams(
        dimension_semantics=("arbitrary",) * n_axes, vmem_limit_bytes=_VMEM_LIMIT)


def _div_tile(n, cap, mult):
    t = (min(cap, n) // mult) * mult
    while t > 0 and n % t:
        t -= mult
    assert t > 0, (n, cap, mult)
    return t


class _Plan(NamedTuple):
    tr: int
    r_up: int
    r_in: int
    n_sub: int
    tn: int
    r_dn: int
    c_dn: int
    tk_dn: int
    r_moe: int
    ts_moe: int
    tf_moe: int
    c_moe: int
    tk_moe: int
    tg: int
    tc: int


def _make_plan(m, ms, d, f, fe):
    r_up = _div_tile(m, 2176, 4 * _BF16_ROWS)
    return _Plan(
        tr=_div_tile(m, 256, 8),
        r_up=r_up,
        r_in=_div_tile(m, 1088, 4 * _BF16_ROWS),
        n_sub=4,
        tn=_div_tile(d, 256, _LANES),
        r_dn=r_up,
        c_dn=_div_tile(d, 1024, _LANES),
        tk_dn=1024 if f >= 1024 else _LANES,
        r_moe=2560 if m >= 2560 else 256,
        ts_moe=512 if m >= 2560 else 64,
        tf_moe=_div_tile(fe, 256, _LANES),
        c_moe=_div_tile(d, 2048, _LANES),
        tk_moe=_div_tile(fe, 512, _LANES),
        tg=256 if m >= 2560 else 64,
        tc=_div_tile(ms, 128, 8),
    )


def _rms(x, g):
    r = lax.rsqrt(jnp.mean(x * x, axis=-1, keepdims=True) + _EPS)
    return x * r * g


def _rmsnorm_body(h_ref, g_ref, o_ref):
    o_ref[...] = _rms(h_ref[...], g_ref[...]).astype(o_ref.dtype)


def _rmsnorm(h, gains, layer, out_dtype, plan, row0=0, rows=None):
    m, d = h.shape
    rows = m if rows is None else rows
    tr = _div_tile(rows, plan.tr, 8)
    assert row0 % tr == 0
    b0 = row0 // tr
    return pl.pallas_call(
        _rmsnorm_body,
        grid=(rows // tr,),
        in_specs=[pl.BlockSpec((tr, d), lambda i: (b0 + i, 0)),
                  pl.BlockSpec((None, 1, d), lambda i: (layer, 0, 0))],
        out_specs=pl.BlockSpec((tr, d), lambda i: (i, 0)),
        out_shape=jax.ShapeDtypeStruct((rows, d), out_dtype),
        compiler_params=_params(1),
        name="rmsnorm",
    )(h, gains)


def _up_body(*refs, n_w, n_x, n_sub, n_valid, epilogue):
    x_ref = refs[0]
    w_refs = refs[1:1 + n_w]
    e_refs = refs[1 + n_w:1 + n_w + n_x]
    o_refs = refs[1 + n_w + n_x:]
    ts = x_ref.shape[0] // n_sub

    def compute():
        ws = [w[...].astype(_BF16) for w in w_refs]
        for s in range(n_sub):
            rows = slice(s * ts, (s + 1) * ts)
            x = x_ref[rows, :]
            accs = [jnp.dot(x, w, preferred_element_type=_F32) for w in ws]
            outs = epilogue(accs, [e[rows, :] for e in e_refs])
            for o, v in zip(o_refs, outs):
                o[rows, :] = v.astype(o.dtype)

    if n_valid is None:
        compute()
    else:
        j = pl.program_id(1)
        pl.when(j < n_valid)(compute)

        @pl.when(j >= n_valid)
        def _():
            for o in o_refs:
                o[...] = jnp.zeros(o.shape, o.dtype)


def _dense_up(x, weights, extras, out_dtypes, epilogue, *, rows, tn, n_sub, n_tiles,
              n_valid=None, name):
    m, k = x.shape
    last = n_tiles - 1 if n_valid is None else n_valid - 1

    def w_spec(layer, off):
        return pl.BlockSpec((None, k, tn), lambda i, j: (layer, 0, off + jnp.minimum(j, last)))

    in_specs = [pl.BlockSpec((rows, k), lambda i, j: (i, 0), pipeline_mode=pl.Buffered(1))]
    in_specs += [w_spec(layer, off) for _, layer, off in weights]
    in_specs += [pl.BlockSpec((rows, tn), lambda i, j: (i, j)) for _ in extras]
    body = functools.partial(_up_body, n_w=len(weights), n_x=len(extras), n_sub=n_sub,
                             n_valid=n_valid, epilogue=epilogue)
    return pl.pallas_call(
        body,
        grid=(m // rows, n_tiles),
        in_specs=in_specs,
        out_specs=[pl.BlockSpec((rows, tn), lambda i, j: (i, j)) for _ in out_dtypes],
        out_shape=[jax.ShapeDtypeStruct((m, n_tiles * tn), dt) for dt in out_dtypes],
        compiler_params=_params(2),
        name=name,
    )(x, *[w for w, _, _ in weights], *extras)


def _ep_conv_in(accs, _):
    b, c, xv = accs
    return b, c * xv


def _ep_swiglu(accs, _):
    g, u = accs
    return (jax.nn.silu(g) * u,)


def _gelu(x):
    return 0.5 * x * (1.0 + lax.erf(x * 0.7071067811865476))


def _ep_gelu2(accs, _):
    u, v = accs
    return _gelu(u), _gelu(v)


def _ep_residual(accs, extras):
    return (extras[0] + accs[0],)


def _down_body(a_ref, w_ref, res_ref, o_ref, *, k_total):
    kk = pl.program_id(2)
    tk = w_ref.shape[0]

    @pl.when(kk == 0)
    def _():
        o_ref[...] = res_ref[...]

    w = w_ref[...]
    if k_total % tk:
        row = lax.broadcasted_iota(jnp.int32, w.shape, 0)
        w = jnp.where(row < k_total - kk * tk, w, 0.0)
    o_ref[...] += jnp.dot(a_ref[...], w.astype(_BF16), preferred_element_type=_F32)


def _dense_down(a, w, layer, res, *, rows, cols, tk, k_total, name):
    m, kp = a.shape
    n = w.shape[-1]
    assert kp % tk == 0 and kp - k_total < tk
    return pl.pallas_call(
        functools.partial(_down_body, k_total=k_total),
        grid=(m // rows, n // cols, kp // tk),
        in_specs=[pl.BlockSpec((rows, tk), lambda i, c, kk: (i, kk)),
                  pl.BlockSpec((None, tk, cols), lambda i, c, kk: (layer, kk, c)),
                  pl.BlockSpec((rows, cols), lambda i, c, kk: (i, c),
                               pipeline_mode=pl.Buffered(1))],
        out_specs=pl.BlockSpec((rows, cols), lambda i, c, kk: (i, c)),
        out_shape=jax.ShapeDtypeStruct((m, n), _F32),
        compiler_params=_params(3),
        name=name,
    )(a, w, res)


def _conv_prompt_body(b_ref, cx_ref, taps_ref, g_ref, st_ref):
    x = cx_ref[...]
    t_len = x.shape[0]
    row = lax.broadcasted_iota(jnp.int32, x.shape, 0)
    x1 = jnp.where(row >= 1, pltpu.roll(x, 1, 0), 0.0)
    x2 = jnp.where(row >= 2, pltpu.roll(x, 2, 0), 0.0)
    tp = taps_ref[...]
    conv = tp[0:1] * x2 + tp[1:2] * x1 + tp[2:3] * x
    g_ref[...] = (b_ref[...] * conv).astype(g_ref.dtype)
    st_ref[...] = x[t_len - 2:t_len, :]


def _conv_sample_body(b_ref, cx_ref, hist_ref, taps_ref, _, g_ref):
    x = cx_ref[...]
    hist = hist_ref[...]
    n = x.shape[0]
    bs = hist.shape[0] // 2
    xcat = jnp.concatenate([hist, x], axis=0)
    tp = taps_ref[...]
    conv = tp[0:1] * xcat[0:n] + tp[1:2] * xcat[bs:bs + n] + tp[2:3] * xcat[2 * bs:2 * bs + n]
    g_ref[...] = (b_ref[...] * conv).astype(g_ref.dtype)


def _conv_mix(b, cx, taps, layer, hist, *, n_seq, seq, ms):
    m, d = b.shape
    mp = n_seq * seq
    ct = _div_tile(d, 512, _LANES)
    g, st = pl.pallas_call(
        _conv_prompt_body,
        grid=(n_seq, d // ct),
        in_specs=[pl.BlockSpec((seq, ct), lambda s, c: (s, c)),
                  pl.BlockSpec((seq, ct), lambda s, c: (s, c)),
                  pl.BlockSpec((None, 3, ct), lambda s, c: (layer, 0, c))],
        out_specs=[pl.BlockSpec((seq, ct), lambda s, c: (s, c)),
                   pl.BlockSpec((None, 2, ct), lambda s, c: (s, 0, c))],
        out_shape=[jax.ShapeDtypeStruct((m, d), _BF16),
                   jax.ShapeDtypeStruct((n_seq, 2, d), _F32)],
        compiler_params=_params(2),
        name="conv_prompt",
    )(b, cx, taps)
    assert mp % ms == 0
    sb = mp // ms
    g = pl.pallas_call(
        _conv_sample_body,
        grid=(d // ct,),
        in_specs=[pl.BlockSpec((ms, ct), lambda c: (sb, c)),
                  pl.BlockSpec((ms, ct), lambda c: (sb, c)),
                  pl.BlockSpec((hist.shape[0], ct), lambda c: (0, c)),
                  pl.BlockSpec((None, 3, ct), lambda c: (layer, 0, c)),
                  pl.BlockSpec(memory_space=pl.ANY)],
        out_specs=pl.BlockSpec((ms, ct), lambda c: (sb, c)),
        out_shape=jax.ShapeDtypeStruct((m, d), _BF16),
        input_output_aliases={4: 0},
        compiler_params=_params(1),
        name="conv_sample",
    )(b, cx, hist, taps, g)
    return g, st


def _layer_norm(v, g, b):
    mu = jnp.mean(v, axis=-1, keepdims=True)
    vc = v - mu
    r = lax.rsqrt(jnp.mean(vc * vc, axis=-1, keepdims=True) + _EPS)
    return vc * r * g + b


def _sgu_prompt_body(u_ref, v_ref, lg_ref, lb_ref, wm_ref, bias_ref, o_ref, *, chunk):
    vn = _layer_norm(v_ref[...], lg_ref[...], lb_ref[...]).astype(_BF16)
    n_groups = wm_ref.shape[0]
    gd = vn.shape[1] // n_groups
    for c in range(vn.shape[0] // chunk):
        rows = slice(c * chunk, (c + 1) * chunk)
        for g in range(n_groups):
            cols = slice(g * gd, (g + 1) * gd)
            z = jnp.dot(wm_ref[g], vn[rows, cols], preferred_element_type=_F32)
            z = z + bias_ref[:, cols]
            o_ref[rows, cols] = (u_ref[rows, cols] * z).astype(o_ref.dtype)


def _sgu_sample_body(u_ref, v_ref, lg_ref, lb_ref, wcol_ref, bcol_ref, _, o_ref, vn_ref,
                     vn_all):
    i = pl.program_id(0)

    @pl.when(i == 0)
    def _():
        vn_all[...] = jnp.zeros(vn_all.shape, vn_all.dtype)

    vn = _layer_norm(v_ref[...], lg_ref[...], lb_ref[...])
    vn_ref[...] = vn
    vn_all[i] = vn
    z = bcol_ref[...]
    for j in range(vn_all.shape[0]):
        z = z + wcol_ref[j:j + 1, :] * vn_all[j]
    o_ref[...] = (u_ref[...] * z).astype(o_ref.dtype)


def _sgu_mix(u, v, ln_g, ln_b, w_sp, b_sp, layer, *, mp, chunk, bs, ts):
    m, d = u.shape
    n_groups = w_sp.shape[1]
    gd = d // n_groups
    tril = jnp.tril(jnp.ones((chunk, chunk), dtype=bool))
    wm = jnp.where(tril[None], w_sp[layer], 0.0)
    bias = jnp.repeat(jnp.transpose(b_sp[layer]), gd, axis=1)
    tr = _div_tile(mp, 2 * chunk, chunk)
    uz = pl.pallas_call(
        functools.partial(_sgu_prompt_body, chunk=chunk),
        grid=(mp // tr,),
        in_specs=[pl.BlockSpec((tr, d), lambda i: (i, 0)),
                  pl.BlockSpec((tr, d), lambda i: (i, 0)),
                  pl.BlockSpec((None, 1, d), lambda i: (layer, 0, 0)),
                  pl.BlockSpec((None, 1, d), lambda i: (layer, 0, 0)),
                  pl.BlockSpec((n_groups, chunk, chunk), lambda i: (0, 0, 0)),
                  pl.BlockSpec((chunk, d), lambda i: (0, 0))],
        out_specs=pl.BlockSpec((tr, d), lambda i: (i, 0)),
        out_shape=jax.ShapeDtypeStruct((m, d), _BF16),
        compiler_params=_params(1),
        name="sgu_prompt",
    )(u, v, ln_g, ln_b, wm.astype(_BF16), bias)
    wcol = jnp.repeat(jnp.transpose(wm[:, :ts, :ts], (1, 2, 0)), gd, axis=-1)
    bcol = bias[:ts].reshape(ts, 1, d)
    assert mp % bs == 0
    sb = mp // bs
    uz, vn = pl.pallas_call(
        _sgu_sample_body,
        grid=(ts,),
        in_specs=[pl.BlockSpec((bs, d), lambda i: (sb + i, 0)),
                  pl.BlockSpec((bs, d), lambda i: (sb + i, 0)),
                  pl.BlockSpec((None, 1, d), lambda i: (layer, 0, 0)),
                  pl.BlockSpec((None, 1, d), lambda i: (layer, 0, 0)),
                  pl.BlockSpec((None, ts, d), lambda i: (i, 0, 0)),
                  pl.BlockSpec((None, 1, d), lambda i: (i, 0, 0)),
                  pl.BlockSpec(memory_space=pl.ANY)],
        out_specs=[pl.BlockSpec((bs, d), lambda i: (sb + i, 0)),
                   pl.BlockSpec((bs, d), lambda i: (i, 0))],
        out_shape=[jax.ShapeDtypeStruct((m, d), _BF16),
                   jax.ShapeDtypeStruct((ts * bs, d), _F32)],
        scratch_shapes=[pltpu.VMEM((ts, bs, d), _F32)],
        input_output_aliases={6: 0},
        compiler_params=_params(1),
        name="sgu_sample",
    )(u, v, ln_g, ln_b, wcol, bcol, uz)
    return uz, vn


def _pack_bf16_pairs(x_bf16):
    bits = lax.bitcast_convert_type(x_bf16.astype(_F32), jnp.uint32)
    half = bits.shape[1] // 2
    return bits[:, half:] | (bits[:, :half] >> 16)


def _unpack_bf16_pairs(w):
    lo = lax.bitcast_convert_type(w << 16, _F32)
    hi = lax.bitcast_convert_type(w & jnp.uint32(0xFFFF0000), _F32)
    return lo.astype(_BF16), hi.astype(_BF16)


def _router_body(h_ref, g_ref, wr_ref, hn_ref, idx_ref, p_ref, *, n_experts):
    hn = _rms(h_ref[...], g_ref[...]).astype(_BF16)
    hn_ref[...] = _pack_bf16_pairs(hn)
    logits = jnp.dot(hn, wr_ref[...].astype(_BF16), preferred_element_type=_F32)
    lane = lax.broadcasted_iota(jnp.int32, logits.shape, 1)
    neg = jnp.float32(-jnp.inf)
    logits = jnp.where(lane < n_experts, logits, neg)
    big = jnp.int32(logits.shape[1])
    m1 = jnp.max(logits, axis=-1, keepdims=True)
    i1 = jnp.min(jnp.where(logits == m1, lane, big), axis=-1, keepdims=True)
    rest = jnp.where(lane == i1, neg, logits)
    m2 = jnp.max(rest, axis=-1, keepdims=True)
    i2 = jnp.min(jnp.where(rest == m2, lane, big), axis=-1, keepdims=True)
    e2 = jnp.exp(m2 - m1)
    den = 1.0 + e2
    col = lax.broadcasted_iota(jnp.int32, idx_ref.shape, 1)
    idx_ref[...] = jnp.where(col == 0, i1, i2)
    p_ref[...] = jnp.where(col == 0, 1.0 / den, e2 / den)


def _router(h, gains, layer, w_router_padded, wr_layer, n_experts, plan):
    m, d = h.shape
    tr = plan.tr
    return pl.pallas_call(
        functools.partial(_router_body, n_experts=n_experts),
        grid=(m // tr,),
        in_specs=[pl.BlockSpec((tr, d), lambda i: (i, 0)),
                  pl.BlockSpec((None, 1, d), lambda i: (layer, 0, 0)),
                  pl.BlockSpec((None, d, _LANES), lambda i: (wr_layer, 0, 0))],
        out_specs=[pl.BlockSpec((tr, d // 2), lambda i: (i, 0)),
                   pl.BlockSpec((tr, _TOP_K), lambda i: (i, 0)),
                   pl.BlockSpec((tr, _TOP_K), lambda i: (i, 0))],
        out_shape=[jax.ShapeDtypeStruct((m, d // 2), jnp.uint32),
                   jax.ShapeDtypeStruct((m, _TOP_K), jnp.int32),
                   jax.ShapeDtypeStruct((m, _TOP_K), _F32)],
        compiler_params=_params(1),
        name="router",
    )(h, gains, w_router_padded)


def _dispatch_body(tok_ref, live_ref, dst_ref, src_ref, o_ref, buf, sem):
    i = pl.program_id(0)
    tg = buf.shape[0]
    base = i * tg

    @pl.when(live_ref[i] > 0)
    def _():
        def issue(pair, carry):
            for prio in range(2):
                r = pair * 2 + prio
                t = tok_ref[base + r]
                pltpu.make_async_copy(src_ref.at[pl.ds(t, 1), :], buf.at[pl.ds(r, 1), :],
                                      sem).start(priority=prio)
            return carry

        lax.fori_loop(0, tg // 2, issue, 0)
        pltpu.make_async_copy(src_ref.at[pl.ds(0, tg), :], buf, sem).wait()
        lo, hi = _unpack_bf16_pairs(buf[...])
        half = buf.shape[1]
        o_ref[:, :half] = lo
        o_ref[:, half:] = hi


def _dispatch(hn_packed, row_token, tile_live, tile_dst, p_rows, plan):
    m, half = hn_packed.shape
    tg = plan.tg
    return pl.pallas_call(
        _dispatch_body,
        grid_spec=pltpu.PrefetchScalarGridSpec(
            num_scalar_prefetch=3,
            grid=(p_rows // tg,),
            in_specs=[pl.BlockSpec(memory_space=pl.ANY)],
            out_specs=pl.BlockSpec((tg, 2 * half), lambda i, tok, live, dst: (dst[i], 0)),
            scratch_shapes=[pltpu.VMEM((tg, half), jnp.uint32), pltpu.SemaphoreType.DMA(())]),
        out_shape=jax.ShapeDtypeStruct((p_rows, 2 * half), _BF16),
        compiler_params=_params(1),
        name="moe_dispatch",
    )(row_token, tile_live, tile_dst, hn_packed)


def _moe_up_body(blk_ref, exp_ref, nsub_ref, x_ref, wg_ref, wu_ref, o_ref, *, ts):
    i = pl.program_id(0)
    nsub = nsub_ref[i]
    for n in range(1, x_ref.shape[0] // ts + 1):
        @pl.when(nsub == n)
        def _(n=n):
            x = x_ref[0:n * ts, :]
            g = jnp.dot(x, wg_ref[...].astype(_BF16), preferred_element_type=_F32)
            u = jnp.dot(x, wu_ref[...].astype(_BF16), preferred_element_type=_F32)
            o_ref[0:n * ts, :] = (jax.nn.silu(g) * u).astype(o_ref.dtype)


def _moe_up(xs, wg, wu, layer, blk, exp, nsub, plan):
    p_rows, d = xs.shape
    fe = wg.shape[-1]
    r, tf = plan.r_moe, plan.tf_moe
    nf = fe // tf

    def fcol(i, f, nsub):
        return jnp.where(nsub[i] > 0, f, nf - 1)

    return pl.pallas_call(
        functools.partial(_moe_up_body, ts=plan.ts_moe),
        grid_spec=pltpu.PrefetchScalarGridSpec(
            num_scalar_prefetch=3,
            grid=(blk.shape[0], nf),
            in_specs=[pl.BlockSpec((r, d), lambda i, f, blk, exp, nsub: (blk[i], 0),
                                   pipeline_mode=pl.Buffered(1)),
                      pl.BlockSpec((None, None, d, tf),
                                   lambda i, f, blk, exp, nsub: (layer, exp[i], 0, fcol(i, f, nsub))),
                      pl.BlockSpec((None, None, d, tf),
                                   lambda i, f, blk, exp, nsub: (layer, exp[i], 0, fcol(i, f, nsub)))],
            out_specs=pl.BlockSpec((r, tf), lambda i, f, blk, exp, nsub: (blk[i], fcol(i, f, nsub)))),
        out_shape=jax.ShapeDtypeStruct((p_rows, fe), _BF16),
        compiler_params=_params(2),
        name="moe_up",
    )(blk, exp, nsub, xs, wg, wu)


def _moe_down_body(blk_ref, exp_ref, nsub_ref, a_ref, w_ref, o_ref, *, ts):
    i = pl.program_id(0)
    kk = pl.program_id(2)
    nsub = nsub_ref[i]

    @pl.when(jnp.logical_and(kk == 0, nsub > 0))
    def _():
        o_ref[...] = jnp.zeros(o_ref.shape, o_ref.dtype)

    for n in range(1, a_ref.shape[0] // ts + 1):
        @pl.when(nsub == n)
        def _(n=n):
            o_ref[0:n * ts, :] += jnp.dot(a_ref[0:n * ts, :], w_ref[...].astype(_BF16),
                                          preferred_element_type=_F32)


def _moe_down(a, wd, layer, blk, exp, nsub, plan):
    p_rows, fe = a.shape
    d = wd.shape[-1]
    r, c, tk = plan.r_moe, plan.c_moe, plan.tk_moe
    nc, nk = d // c, fe // tk

    def live(i, v, last, nsub):
        return jnp.where(nsub[i] > 0, v, last)

    return pl.pallas_call(
        functools.partial(_moe_down_body, ts=plan.ts_moe),
        grid_spec=pltpu.PrefetchScalarGridSpec(
            num_scalar_prefetch=3,
            grid=(blk.shape[0], nc, nk),
            in_specs=[pl.BlockSpec((r, tk), lambda i, c_, k_, blk, exp, nsub:
                                   (blk[i], live(i, k_, nk - 1, nsub))),
                      pl.BlockSpec((None, None, tk, c), lambda i, c_, k_, blk, exp, nsub:
                                   (layer, exp[i], live(i, k_, nk - 1, nsub),
                                    live(i, c_, nc - 1, nsub)))],
            out_specs=pl.BlockSpec((r, c), lambda i, c_, k_, blk, exp, nsub:
                                   (blk[i], live(i, c_, nc - 1, nsub)),
                                   pipeline_mode=pl.Buffered(1))),
        out_shape=jax.ShapeDtypeStruct((p_rows, d), _F32),
        compiler_params=_params(3),
        name="moe_down",
    )(blk, exp, nsub, a, wd)


def _combine_rows(pos_ref, h_ref, p_ref, y_ref, buf, sem):
    tc = h_ref.shape[0]
    base = pl.program_id(0) * tc

    def issue(r, carry):
        for k in range(_TOP_K):
            p = pos_ref[(base + r) * _TOP_K + k]
            pltpu.make_async_copy(y_ref.at[pl.ds(p, 1), :], buf.at[k, pl.ds(r, 1), :],
                                  sem).start(priority=k)
        return carry

    lax.fori_loop(0, tc, issue, 0)
    for k in range(_TOP_K):
        pltpu.make_async_copy(y_ref.at[pl.ds(0, tc), :], buf.at[k], sem).wait()
    p = p_ref[...]
    return h_ref[...] + (p[:, 0:1] * buf[0] + p[:, 1:2] * buf[1])


def _combine_mid_body(pos_ref, h_ref, p_ref, g_ref, y_ref, o_ref, hn_ref, buf, sem):
    h = _combine_rows(pos_ref, h_ref, p_ref, y_ref, buf, sem)
    o_ref[...] = h
    hn_ref[...] = _rms(h, g_ref[...]).astype(hn_ref.dtype)


def _combine_last_body(pos_ref, h_ref, p_ref, g_ref, y_ref, op_ref, os_ref, buf, sem, *, n_p):
    out = _rms(_combine_rows(pos_ref, h_ref, p_ref, y_ref, buf, sem), g_ref[...])
    i = pl.program_id(0)

    @pl.when(i < n_p)
    def _():
        op_ref[...] = out

    @pl.when(i >= n_p)
    def _():
        os_ref[...] = out


def _combine(h, gate_p, y, pos, gains, layer, plan, *, mp=None):
    m, d = h.shape
    tc = plan.tc
    in_specs = [pl.BlockSpec((tc, d), lambda i, pos: (i, 0)),
                pl.BlockSpec((tc, _TOP_K), lambda i, pos: (i, 0)),
                pl.BlockSpec((None, 1, d), lambda i, pos: (layer, 0, 0)),
                pl.BlockSpec(memory_space=pl.ANY)]
    scratch = [pltpu.VMEM((_TOP_K, tc, d), _F32), pltpu.SemaphoreType.DMA(())]
    if mp is None:
        body = _combine_mid_body
        out_specs = [pl.BlockSpec((tc, d), lambda i, pos: (i, 0)),
                     pl.BlockSpec((tc, d), lambda i, pos: (i, 0))]
        out_shape = [jax.ShapeDtypeStruct((m, d), _F32), jax.ShapeDtypeStruct((m, d), _BF16)]
    else:
        n_p = mp // tc
        body = functools.partial(_combine_last_body, n_p=n_p)
        out_specs = [pl.BlockSpec((tc, d), lambda i, pos: (jnp.minimum(i, n_p - 1), 0)),
                     pl.BlockSpec((tc, d), lambda i, pos: (jnp.maximum(i - n_p, 0), 0))]
        out_shape = [jax.ShapeDtypeStruct((mp, d), _F32), jax.ShapeDtypeStruct((m - mp, d), _F32)]
    return pl.pallas_call(
        body,
        grid_spec=pltpu.PrefetchScalarGridSpec(
            num_scalar_prefetch=1, grid=(m // tc,), in_specs=in_specs, out_specs=out_specs,
            scratch_shapes=scratch),
        out_shape=out_shape,
        compiler_params=_params(1),
        name="moe_combine",
    )(pos, h, gate_p, gains, y)


def _route(top_i, n_experts, plan, n_blocks):
    r, ts, tg = plan.r_moe, plan.ts_moe, plan.tg
    e_flat = top_i.reshape(-1)
    onehot = (e_flat[:, None] == jnp.arange(n_experts, dtype=jnp.int32)[None]).astype(jnp.int32)
    csum = jnp.cumsum(onehot, axis=0)
    rank = jnp.sum((csum - onehot) * onehot, axis=1)
    counts = csum[-1]
    subs = (counts + ts - 1) // ts
    nblk = (subs + r // ts - 1) // (r // ts)
    q = ((subs + jnp.maximum(nblk, 1) - 1) // jnp.maximum(nblk, 1)) * ts
    q = jnp.maximum(q, ts)
    blk_end = jnp.cumsum(nblk)
    blk_base = blk_end - nblk
    q_a = q[e_flat]
    pos = (blk_base[e_flat] + rank // q_a) * r + rank % q_a
    p_rows = n_blocks * r
    row_token = jnp.zeros((p_rows,), jnp.int32).at[pos].set(
        jnp.arange(e_flat.shape[0], dtype=jnp.int32) // _TOP_K)
    bidx = jnp.arange(n_blocks, dtype=jnp.int32)
    blk_exp = jnp.sum((bidx[:, None] >= blk_end[None]).astype(jnp.int32), axis=1)
    live = bidx < blk_end[-1]
    blk_exp = jnp.where(live, jnp.minimum(blk_exp, n_experts - 1), 0)
    q_b = q[blk_exp]
    in_blk = jnp.clip(counts[blk_exp] - (bidx - blk_base[blk_exp]) * q_b, 0, q_b)
    nsub = jnp.where(live, (in_blk + ts - 1) // ts, 0).astype(jnp.int32)
    last = jnp.maximum(blk_end[-1] - 1, 0)
    blk_of = jnp.where(live, bidx, last)
    blk_exp = jnp.where(live, blk_exp, blk_exp[last])
    tidx = jnp.arange(p_rows // tg, dtype=jnp.int32)
    tile_live = ((tidx * tg) % r < nsub[(tidx * tg) // r] * ts).astype(jnp.int32)
    tile_dst = lax.cummax(jnp.where(tile_live > 0, tidx, 0), axis=0)
    return row_token, pos, blk_of, blk_exp, nsub, tile_live, tile_dst


def kernel(x_prompt, x_sample, state_conv, norm_mix_g, norm_ffn_g, norm_final_g, conv_w_in, conv_taps, conv_w_out, sgu_w_in, sgu_ln_g, sgu_ln_b, sgu_w_spatial, sgu_b_spatial, sgu_w_out, ffn_w_gate, ffn_w_up, ffn_w_down, moe_router, moe_w_gate, moe_w_up, moe_w_down):
    n_seq, seq, d = x_prompt.shape
    bs, ts, _ = x_sample.shape
    depth = norm_mix_g.shape[0]
    chunk = sgu_w_spatial.shape[-1]
    f = ffn_w_gate.shape[-1]
    n_experts, fe = moe_w_gate.shape[1], moe_w_gate.shape[-1]
    mp, ms = n_seq * seq, bs * ts
    m = mp + ms
    assert conv_taps.shape[1] == 3 and ts >= 2 and seq % chunk == 0 and ts <= chunk
    plan = _make_plan(m, ms, d, f, fe)

    h = jnp.concatenate([x_prompt.reshape(mp, d),
                         jnp.transpose(x_sample, (1, 0, 2)).reshape(ms, d)], axis=0)
    g_mix = norm_mix_g.reshape(depth, 1, d)
    g_ffn = norm_ffn_g.reshape(depth, 1, d)
    g_fin = norm_final_g.reshape(1, 1, d)
    ln_g = sgu_ln_g.reshape(-1, 1, d)
    ln_b = sgu_ln_b.reshape(-1, 1, d)
    w_router = jnp.pad(moe_router, ((0, 0), (0, 0), (0, _LANES - n_experts)))
    nt = d // plan.tn
    n_blocks = (m * _TOP_K) // plan.r_moe + n_experts
    f_pad = -(-f // plan.tk_dn) * plan.tk_dn

    conv_p, conv_s, sgu_v = [], [], []
    hn_next = y_final = None
    for i in range(depth):
        j = i // 2
        hn = _rmsnorm(h, g_mix, i, _BF16, plan) if hn_next is None else hn_next
        hn_next = None
        if i % 2 == 0:
            b, cx = _dense_up(
                hn, [(conv_w_in, j, 0), (conv_w_in, j, nt), (conv_w_in, j, 2 * nt)], [],
                [_F32, _F32], _ep_conv_in, rows=plan.r_in, tn=plan.tn, n_sub=plan.n_sub,
                n_tiles=nt, name="conv_in")
            hist = jnp.transpose(state_conv[j], (1, 0, 2)).reshape(2 * bs, d)
            g, st = _conv_mix(b, cx, conv_taps, j, hist, n_seq=n_seq, seq=seq, ms=ms)
            conv_p.append(st)
            conv_s.append(jnp.transpose(cx[m - 2 * bs:].reshape(2, bs, d), (1, 0, 2)))
            (h,) = _dense_up(g, [(conv_w_out, j, 0)], [h], [_F32], _ep_residual,
                             rows=plan.r_up, tn=plan.tn, n_sub=plan.n_sub, n_tiles=nt,
                             name="conv_out")
        else:
            u, v = _dense_up(
                hn, [(sgu_w_in, j, 0), (sgu_w_in, j, nt)], [], [_F32, _F32], _ep_gelu2,
                rows=plan.r_up, tn=plan.tn, n_sub=plan.n_sub, n_tiles=nt, name="sgu_in")
            uz, vn = _sgu_mix(u, v, ln_g, ln_b, sgu_w_spatial, sgu_b_spatial, j,
                              mp=mp, chunk=chunk, bs=bs, ts=ts)
            sgu_v.append(jnp.transpose(vn.reshape(ts, bs, d), (1, 0, 2)))
            (h,) = _dense_up(uz, [(sgu_w_out, j, 0)], [h], [_F32], _ep_residual,
                             rows=plan.r_up, tn=plan.tn, n_sub=plan.n_sub, n_tiles=nt,
                             name="sgu_out")
        if i % 2 == 0:
            hn = _rmsnorm(h, g_ffn, i, _BF16, plan)
            (a,) = _dense_up(
                hn, [(ffn_w_gate, j, 0), (ffn_w_up, j, 0)], [], [_BF16], _ep_swiglu,
                rows=plan.r_up, tn=plan.tn, n_sub=plan.n_sub, n_tiles=f_pad // plan.tn,
                n_valid=f // plan.tn, name="ffn_up")
            h = _dense_down(a, ffn_w_down, j, h, rows=plan.r_dn, cols=plan.c_dn,
                            tk=plan.tk_dn, k_total=f, name="ffn_down")
        else:
            hn_packed, top_i, top_p = _router(h, g_ffn, i, w_router, j, n_experts, plan)
            row_token, pos, blk_of, blk_exp, nsub, tile_live, tile_dst = _route(
                top_i, n_experts, plan, n_blocks)
            xs = _dispatch(hn_packed, row_token, tile_live, tile_dst, n_blocks * plan.r_moe, plan)
            a = _moe_up(xs, moe_w_gate, moe_w_up, j, blk_of, blk_exp, nsub, plan)
            y = _moe_down(a, moe_w_down, j, blk_of, blk_exp, nsub, plan)
            if i + 1 < depth:
                h, hn_next = _combine(h, top_p, y, pos, g_mix, i + 1, plan)
            else:
                y_final = _combine(h, top_p, y, pos, g_fin, 0, plan, mp=mp)

    if y_final is None:
        y_final = (_rmsnorm(h, g_fin, 0, _F32, plan, row0=0, rows=mp),
                   _rmsnorm(h, g_fin, 0, _F32, plan, row0=mp, rows=ms))
    y_prompt = y_final[0].reshape(n_seq, seq, d)
    y_sample = jnp.transpose(y_final[1].reshape(ts, bs, d), (1, 0, 2))
    return (y_prompt, y_sample, jnp.stack(conv_p), jnp.stack(conv_s), jnp.stack(sgu_v))
```
